```python
import jax, jax.numpy as jnp
from jax import lax
import numpy as np

D_MODEL = 1024
BATCH = 16
SEQ = 4096
DEPTH = 2

SSD_EXPAND = 2
SSD_D_INNER = SSD_EXPAND * D_MODEL
SSD_HEAD_DIM = 64
SSD_N_HEADS = SSD_D_INNER // SSD_HEAD_DIM
SSD_N_GROUPS = 4
SSD_HEADS_PER_GROUP = SSD_N_HEADS // SSD_N_GROUPS
SSD_D_STATE = 128
SSD_CONV_WIDTH = 4
SSD_CHUNK = 128
SSD_CONV_DIM = SSD_D_INNER + 2 * SSD_N_GROUPS * SSD_D_STATE
DT_MIN = 1e-3
DT_MAX = 1e-1

SC_WIDTH = D_MODEL
SC_CONV_WIDTH = 3

N_BRANCHES = 2

D_FF = 2816
FFN_CONV_WIDTH = 3

EPS = 1e-6

IN_SPLIT_SIZES = (SSD_D_INNER, SSD_CONV_DIM, SSD_N_HEADS,
                  SC_WIDTH, SC_WIDTH, SC_WIDTH, N_BRANCHES * D_MODEL)
D_IN_PROJ = sum(IN_SPLIT_SIZES)

kernel_name = "hybrid_ssd_shortconv_adaln_block"


def rmsnorm(x, g):
    xf = x.astype(jnp.float32)
    xf = xf * lax.rsqrt(jnp.mean(xf * xf, axis=-1, keepdims=True) + EPS)
    return xf.astype(x.dtype) * g


def grouped_rmsnorm(y, g, n_groups):
    shp = y.shape
    yf = y.astype(jnp.float32).reshape(*shp[:-1], n_groups, shp[-1] // n_groups)
    yf = yf * lax.rsqrt(jnp.mean(yf * yf, axis=-1, keepdims=True) + EPS)
    return yf.reshape(shp) * g


def causal_dwconv(u, w, b=None):
    k_width = w.shape[0]
    seqlen = u.shape[1]
    up = jnp.pad(u, ((0, 0), (k_width - 1, 0), (0, 0)))
    y = up[:, 0:seqlen] * w[0]
    for k in range(1, k_width):
        y = y + up[:, k:k + seqlen] * w[k]
    if b is not None:
        y = y + b
    return y


def ssd_chunked(xh, dt, a, bmat, cmat):
    bsz, seqlen = xh.shape[0], xh.shape[1]
    nc = seqlen // SSD_CHUNK
    L, G, R, P, N = SSD_CHUNK, SSD_N_GROUPS, SSD_HEADS_PER_GROUP, SSD_HEAD_DIM, SSD_D_STATE
    x = (xh * dt[..., None]).reshape(bsz, nc, L, G, R, P)
    adt = jnp.moveaxis((dt * a).reshape(bsz, nc, L, G, R), 2, -1)
    a_cs = jnp.cumsum(adt, axis=-1)
    bc = bmat.reshape(bsz, nc, L, G, N)
    cc = cmat.reshape(bsz, nc, L, G, N)

    causal = jnp.tril(jnp.ones((L, L), dtype=bool))
    decay = jnp.exp(jnp.where(causal, a_cs[..., :, None] - a_cs[..., None, :], -jnp.inf))
    scores = jnp.einsum("bclgn,bcsgn->bcgls", cc, bc)
    m = scores[:, :, :, None] * decay
    y_diag = jnp.einsum("bcgrls,bcsgrp->bclgrp", m, x)

    decay_states = jnp.exp(a_cs[..., -1:] - a_cs)
    states = jnp.einsum("bclgn,bcgrl,bclgrp->bcgrpn", bc, decay_states, x)
    chunk_decay = jnp.exp(a_cs[..., -1])

    def step(h, inp):
        s_c, d_c = inp
        h_new = h * d_c[..., None, None] + s_c
        return h_new, h

    h0 = jnp.zeros((bsz, G, R, P, N), dtype=states.dtype)
    _, prev = lax.scan(step, h0, (jnp.moveaxis(states, 1, 0), jnp.moveaxis(chunk_decay, 1, 0)))
    prev = jnp.moveaxis(prev, 0, 1)

    y_off = jnp.einsum("bclgn,bcgrpn,bcgrl->bclgrp", cc, prev, jnp.exp(a_cs))
    return (y_diag + y_off).reshape(bsz, seqlen, SSD_N_HEADS, P)


def ssd_branch(z, xbc, dt_raw, conv_w, conv_b, dt_bias, a_log, d_skip, norm_g):
    bsz, seqlen = z.shape[0], z.shape[1]
    xbc = jax.nn.silu(causal_dwconv(xbc, conv_w, conv_b))
    gn = SSD_N_GROUPS * SSD_D_STATE
    xs, bm, cm = jnp.split(xbc, [SSD_D_INNER, SSD_D_INNER + gn], axis=-1)
    xh = xs.reshape(bsz, seqlen, SSD_N_HEADS, SSD_HEAD_DIM)
    dt = jax.nn.softplus(dt_raw.astype(jnp.float32) + dt_bias.astype(jnp.float32))
    a = -jnp.exp(a_log.astype(jnp.float32))
    y = ssd_chunked(xh.astype(jnp.float32), dt, a,
                    bm.reshape(bsz, seqlen, SSD_N_GROUPS, SSD_D_STATE).astype(jnp.float32),
                    cm.reshape(bsz, seqlen, SSD_N_GROUPS, SSD_D_STATE).astype(jnp.float32))
    y = y + d_skip.astype(jnp.float32)[:, None] * xh.astype(jnp.float32)
    y = y.reshape(bsz, seqlen, SSD_D_INNER) * jax.nn.silu(z.astype(jnp.float32))
    return grouped_rmsnorm(y, norm_g, SSD_N_GROUPS).astype(z.dtype)


def short_conv_branch(b_gate, c_gate, h, conv_w):
    return b_gate * causal_dwconv(c_gate * h, conv_w)


def hybrid_layer(x, c_act, ada_w, ada_b, mix_pre_g, mix_post_g, w_in,
                 ssd_conv_w, ssd_conv_b, ssd_dt_bias, ssd_a_log, ssd_d, ssd_norm_g,
                 w_ssd_out, sc_conv_w, w_sc_out, w_o,
                 ffn_pre_g, ffn_post_g, w_up, ffn_conv_w, ffn_conv_b, w_down):
    mod = c_act @ ada_w + ada_b
    sh1, sc1, gt1, sh2, sc2, gt2 = [m[:, None, :] for m in jnp.split(mod, 6, axis=-1)]

    h = rmsnorm(x, mix_pre_g) * (1.0 + sc1) + sh1
    proj = h @ w_in
    points, acc = [], 0
    for s in IN_SPLIT_SIZES[:-1]:
        acc += s
        points.append(acc)
    z, xbc, dt_raw, sc_b, sc_c, sc_h, gates = jnp.split(proj, points, axis=-1)
    y_ssd = ssd_branch(z, xbc, dt_raw, ssd_conv_w, ssd_conv_b, ssd_dt_bias,
                       ssd_a_log, ssd_d, ssd_norm_g) @ w_ssd_out
    y_sc = short_conv_branch(sc_b, sc_c, sc_h, sc_conv_w) @ w_sc_out
    g_ssd, g_sc = jnp.split(jax.nn.sigmoid(gates), 2, axis=-1)
    mix = (g_ssd * y_ssd + g_sc * y_sc) @ w_o
    x = x + gt1 * rmsnorm(mix, mix_post_g)

    h = rmsnorm(x, ffn_pre_g) * (1.0 + sc2) + sh2
    u = causal_dwconv(h @ w_up, ffn_conv_w, ffn_conv_b)
    u_gate, u_val = jnp.split(u, 2, axis=-1)
    f = (jax.nn.silu(u_gate) * u_val) @ w_down
    x = x + gt2 * rmsnorm(f, ffn_post_g)
    return x


def setup_inputs(seed: int = 0) -> dict:
    key = jax.random.key(seed)
    ks = jax.random.split(key, 32)
    f32 = jnp.float32

    def dense(k, fan_in, fan_out, scale=1.0):
        return jax.random.normal(k, (DEPTH, fan_in, fan_out), f32) * (scale * fan_in ** -0.5)

    def gain(k, n):
        return 1.0 + 0.05 * jax.random.normal(k, (DEPTH, n), f32)

    def small(k, shape, s=0.02):
        return s * jax.random.normal(k, shape, f32)

    u = jax.random.uniform(ks[8], (DEPTH, SSD_N_HEADS), f32)
    dt0 = jnp.exp(u * (np.log(DT_MAX) - np.log(DT_MIN)) + np.log(DT_MIN))
    dt_bias = dt0 + jnp.log(-jnp.expm1(-dt0))
    a_log = jnp.log(jax.random.uniform(ks[9], (DEPTH, SSD_N_HEADS), f32, 1.0, 16.0))

    return {
        "x": jax.random.normal(ks[0], (BATCH, SEQ, D_MODEL), f32),
        "c": jax.random.normal(ks[1], (BATCH, D_MODEL), f32),
        "ada_w": dense(ks[2], D_MODEL, 6 * D_MODEL, 0.5),
        "ada_b": small(ks[3], (DEPTH, 6 * D_MODEL)),
        "mix_pre_g": gain(ks[4], D_MODEL),
        "mix_post_g": gain(ks[5], D_MODEL),
        "w_in": dense(ks[6], D_MODEL, D_IN_PROJ),
        "ssd_conv_w": jax.random.normal(ks[7], (DEPTH, SSD_CONV_WIDTH, SSD_CONV_DIM), f32) * SSD_CONV_WIDTH ** -0.5,
        "ssd_conv_b": small(ks[10], (DEPTH, SSD_CONV_DIM)),
        "ssd_dt_bias": dt_bias,
        "ssd_a_log": a_log,
        "ssd_d": 1.0 + 0.1 * jax.random.normal(ks[11], (DEPTH, SSD_N_HEADS), f32),
        "ssd_norm_g": gain(ks[12], SSD_D_INNER),
        "w_ssd_out": dense(ks[13], SSD_D_INNER, D_MODEL),
        "sc_conv_w": jax.random.normal(ks[14], (DEPTH, SC_CONV_WIDTH, SC_WIDTH), f32) * SC_CONV_WIDTH ** -0.5,
        "w_sc_out": dense(ks[15], SC_WIDTH, D_MODEL),
        "w_o": dense(ks[16], D_MODEL, D_MODEL),
        "ffn_pre_g": gain(ks[17], D_MODEL),
        "ffn_post_g": gain(ks[18], D_MODEL),
        "w_up": dense(ks[19], D_MODEL, 2 * D_FF),
        "ffn_conv_w": jax.random.normal(ks[20], (DEPTH, FFN_CONV_WIDTH, 2 * D_FF), f32) * FFN_CONV_WIDTH ** -0.5,
        "ffn_conv_b": small(ks[21], (DEPTH, 2 * D_FF)),
        "w_down": dense(ks[22], D_FF, D_MODEL),
    }


def reference(x, c, ada_w, ada_b, mix_pre_g, mix_post_g, w_in,
              ssd_conv_w, ssd_conv_b, ssd_dt_bias, ssd_a_log, ssd_d, ssd_norm_g,
              w_ssd_out, sc_conv_w, w_sc_out, w_o,
              ffn_pre_g, ffn_post_g, w_up, ffn_conv_w, ffn_conv_b, w_down):
    c_act = jax.nn.silu(c)
    for i in range(DEPTH):
        x = hybrid_layer(x, c_act, ada_w[i], ada_b[i], mix_pre_g[i], mix_post_g[i], w_in[i],
                         ssd_conv_w[i], ssd_conv_b[i], ssd_dt_bias[i], ssd_a_log[i], ssd_d[i],
                         ssd_norm_g[i], w_ssd_out[i], sc_conv_w[i], w_sc_out[i], w_o[i],
                         ffn_pre_g[i], ffn_post_g[i], w_up[i], ffn_conv_w[i], ffn_conv_b[i],
                         w_down[i])
    return x
```

```python
import functools

import jax
import jax.numpy as jnp
from jax import lax
from jax.experimental import pallas as pl
from jax.experimental.pallas import tpu as pltpu

F32 = jnp.float32
BF16 = jnp.bfloat16

EPS = 1e-6
LANE = 128
SUBLANE = 8
VMEM_LIMIT = 56 * 1024 * 1024

HEAD_DIM = 64
N_GROUPS = 4
D_STATE = 128
CHUNK = 128
SSD_CONV_K = 4
SC_CONV_K = 3
FFN_CONV_K = 3


def _dot(a, b):
    return jnp.dot(a, b, preferred_element_type=F32)


def _sigmoid(v):
    return 1.0 / (1.0 + jnp.exp(-v))


def _silu(v):
    return v * _sigmoid(v)


def _split_bf16(v):
    hi = v.astype(BF16)
    lo = (v - hi.astype(F32)).astype(BF16)
    return hi, lo


def _rms(v):
    return v * lax.rsqrt(jnp.mean(v * v, axis=-1, keepdims=True) + EPS)


def _ada_kernel(c_ref, w_ref, b_ref, o_ref):
    ca = _silu(c_ref[...])
    c_hi, c_lo = _split_bf16(ca)
    w_hi, w_lo = _split_bf16(w_ref[...])
    acc = _dot(c_hi, w_hi) + _dot(c_hi, w_lo) + _dot(c_lo, w_hi)
    o_ref[...] = acc + b_ref[...]


def _ada_call(c, ada_w, ada_b):
    depth, d, n = ada_w.shape
    bsz = c.shape[0]
    tn = 1024
    return pl.pallas_call(
        _ada_kernel,
        grid=(depth, n // tn),
        in_specs=[
            pl.BlockSpec((bsz, d), lambda l, j: (0, 0)),
            pl.BlockSpec((None, d, tn), lambda l, j: (l, 0, j)),
            pl.BlockSpec((None, 1, tn), lambda l, j: (l, 0, j)),
        ],
        out_specs=pl.BlockSpec((None, bsz, tn), lambda l, j: (l, 0, j)),
        out_shape=jax.ShapeDtypeStruct((depth, bsz, n), F32),
        compiler_params=pltpu.CompilerParams(
            dimension_semantics=("arbitrary", "arbitrary"), vmem_limit_bytes=VMEM_LIMIT),
        name="ada_mod",
    )(c, ada_w, ada_b.reshape(depth, 1, n))


def _inproj_kernel(x_ref, mod_ref, g_ref, w_ref, wdt_ref, o_ref, dt_ref, h_ref):
    j = pl.program_id(2)

    @pl.when(j == 0)
    def _():
        h = _rms(x_ref[...]) * g_ref[...] * (1.0 + mod_ref[1:2, :]) + mod_ref[0:1, :]
        hb = h.astype(BF16)
        h_ref[...] = hb
        dt_ref[...] = _dot(hb, wdt_ref[...])

    o_ref[...] = _dot(h_ref[...], w_ref[...]).astype(BF16)


def _inproj_call(x, mod3, g, w_main, w_dt, tm, tn):
    bsz, seq, d = x.shape
    n = w_main.shape[1]
    return pl.pallas_call(
        _inproj_kernel,
        grid=(bsz, seq // tm, n // tn),
        in_specs=[
            pl.BlockSpec((None, tm, d), lambda b, i, j: (b, i, 0)),
            pl.BlockSpec((None, 6, d), lambda b, i, j: (b, 0, 0)),
            pl.BlockSpec((1, d), lambda b, i, j: (0, 0)),
            pl.BlockSpec((d, tn), lambda b, i, j: (0, j)),
            pl.BlockSpec((d, LANE), lambda b, i, j: (0, 0)),
        ],
        out_specs=[
            pl.BlockSpec((None, tm, tn), lambda b, i, j: (b, i, j)),
            pl.BlockSpec((None, tm, LANE), lambda b, i, j: (b, i, 0)),
        ],
        out_shape=[
            jax.ShapeDtypeStruct((bsz, seq, n), BF16),
            jax.ShapeDtypeStruct((bsz, seq, LANE), F32),
        ],
        scratch_shapes=[pltpu.VMEM((tm, d), BF16)],
        compiler_params=pltpu.CompilerParams(
            dimension_semantics=("parallel", "parallel", "arbitrary"), vmem_limit_bytes=VMEM_LIMIT),
        name="inproj",
    )(x, mod3, g, w_main, w_dt)


def _ssd_kernel(z_ref, xr_ref, bcr_ref, dtr_ref, cw_ref, cb_ref, dtb_ref, a_ref, dexp_ref, ng_ref,
                tril_ref, exp_ref, o_ref,
                raw_ref, xc_ref, state_ref, acs_ref, acst_ref, dtt_ref, ea_ref, y_ref,
                *, tc, d_inner):
    n_heads = d_inner // HEAD_DIM
    hpg = n_heads // N_GROUPS
    gw = hpg * HEAD_DIM
    d_conv = d_inner + 2 * N_GROUPS * D_STATE
    s_idx = pl.program_id(1)

    @pl.when(s_idx == 0)
    def _():
        raw_ref[0:SUBLANE, :] = jnp.zeros((SUBLANE, d_conv), F32)
        state_ref[...] = jnp.zeros_like(state_ref)

    @pl.when(s_idx != 0)
    def _():
        raw_ref[0:SUBLANE, :] = raw_ref[tc:tc + SUBLANE, :]

    raw_ref[SUBLANE:SUBLANE + tc, 0:d_inner] = xr_ref[...].astype(F32)
    raw_ref[SUBLANE:SUBLANE + tc, d_inner:d_conv] = bcr_ref[...].astype(F32)

    slab = 512
    for c0 in range(0, d_conv, slab):
        acc = cb_ref[:, c0:c0 + slab]
        for k in range(SSD_CONV_K):
            off = SUBLANE - (SSD_CONV_K - 1) + k
            acc = acc + raw_ref[off:off + tc, c0:c0 + slab] * cw_ref[k:k + 1, c0:c0 + slab]
        xc_ref[:, c0:c0 + slab] = _silu(acc)

    row = lax.broadcasted_iota(jnp.int32, (CHUNK, CHUNK), 0)
    col = lax.broadcasted_iota(jnp.int32, (CHUNK, CHUNK), 1)
    causal = row >= col
    low_half = col < HEAD_DIM
    b_off = d_inner
    c_off = d_inner + N_GROUPS * D_STATE

    def chunk_body(ci, carry):
        r0 = pl.multiple_of(ci * CHUNK, CHUNK)
        rows = pl.ds(r0, CHUNK)

        dtv = dtr_ref[rows, :] + dtb_ref[...]
        dt = jnp.maximum(dtv, 0.0) + jnp.log(1.0 + jnp.exp(-jnp.abs(dtv)))
        adt_hi, adt_lo = _split_bf16(dt * a_ref[...])
        a_cs = _dot(tril_ref[...], jnp.concatenate([adt_hi, adt_lo], axis=0))
        a_last = a_cs[CHUNK - 1:CHUNK, :]
        w_state = jnp.exp(a_last - a_cs) * dt
        cd = jnp.broadcast_to(jnp.exp(a_last), (SUBLANE, LANE))
        acs_ref[...] = a_cs
        ea_ref[...] = jnp.exp(a_cs)
        acst_ref[...] = a_cs.T
        dtt_ref[...] = dt.T

        f_hi, f_lo = _split_bf16(jnp.concatenate([w_state, cd], axis=0))
        fac = _dot(jnp.concatenate([f_hi, f_lo], axis=1), exp_ref[...])
        w_exp = fac[0:CHUNK, :]
        cd_exp = fac[CHUNK:CHUNK + 1, :]

        for g in range(N_GROUPS):
            bg = xc_ref[rows, b_off + g * D_STATE:b_off + (g + 1) * D_STATE]
            cg = xc_ref[rows, c_off + g * D_STATE:c_off + (g + 1) * D_STATE]
            bb = bg.astype(BF16)
            scores = lax.dot_general(cg.astype(BF16), bb, (((1,), (1,)), ((), ())),
                                     preferred_element_type=F32)
            scores = jnp.where(causal, scores, 0.0)
            xg = xc_ref[rows, g * gw:(g + 1) * gw]
            hg = state_ref[:, g * gw:(g + 1) * gw]
            for j in range(hpg // 2):
                lanes = slice(j * LANE, (j + 1) * LANE)
                rhs = jnp.concatenate([xg[:, lanes].astype(BF16), hg[:, lanes].astype(BF16)], axis=0)
                outs = []
                for q in range(2):
                    r = g * hpg + 2 * j + q
                    a_col = acs_ref[:, r:r + 1]
                    decay = jnp.exp(jnp.minimum(a_col - acst_ref[r:r + 1, :], 0.0))
                    m = scores * decay * dtt_ref[r:r + 1, :]
                    ce = cg * ea_ref[:, r:r + 1]
                    lhs = jnp.concatenate([m.astype(BF16), ce.astype(BF16)], axis=1)
                    outs.append(_dot(lhs, rhs))
                y_pair = jnp.where(low_half, outs[0], outs[1])
                gl = slice(g * gw + j * LANE, g * gw + (j + 1) * LANE)
                y_ref[:, gl] = y_pair + dexp_ref[:, gl] * xg[:, lanes]
            xw = (xg * w_exp[:, g * gw:(g + 1) * gw]).astype(BF16)
            upd = _dot(bg.T.astype(BF16), xw)
            state_ref[:, g * gw:(g + 1) * gw] = cd_exp[:, g * gw:(g + 1) * gw] * hg + upd

        y = y_ref[...] * _silu(z_ref[rows, :].astype(F32))
        for g in range(N_GROUPS):
            yg = y[:, g * gw:(g + 1) * gw]
            o_ref[rows, g * gw:(g + 1) * gw] = (_rms(yg) * ng_ref[:, g * gw:(g + 1) * gw]).astype(BF16)
        return carry

    lax.fori_loop(0, tc // CHUNK, chunk_body, 0)


def _ssd_call(proj, dt_raw, conv_w, conv_b, dt_bias_row, a_row, d_exp, norm_g, tril2, expand2, tc, d_inner):
    bsz, seq, _ = proj.shape
    d_bc = 2 * N_GROUPS * D_STATE
    d_conv = d_inner + d_bc
    kern = functools.partial(_ssd_kernel, tc=tc, d_inner=d_inner)
    const = lambda b, s: (0, 0)
    return pl.pallas_call(
        kern,
        grid=(bsz, seq // tc),
        in_specs=[
            pl.BlockSpec((None, tc, d_inner), lambda b, s: (b, s, 0)),
            pl.BlockSpec((None, tc, d_inner), lambda b, s: (b, s, 1)),
            pl.BlockSpec((None, tc, d_bc), lambda b, s: (b, s, 2 * d_inner // d_bc)),
            pl.BlockSpec((None, tc, LANE), lambda b, s: (b, s, 0)),
            pl.BlockSpec((SSD_CONV_K, d_conv), const),
            pl.BlockSpec((1, d_conv), const),
            pl.BlockSpec((1, LANE), const),
            pl.BlockSpec((1, LANE), const),
            pl.BlockSpec((1, d_inner), const),
            pl.BlockSpec((1, d_inner), const),
            pl.BlockSpec((CHUNK, 2 * CHUNK), const),
            pl.BlockSpec((2 * LANE, d_inner), const),
        ],
        out_specs=pl.BlockSpec((None, tc, d_inner), lambda b, s: (b, s, 0)),
        out_shape=jax.ShapeDtypeStruct((bsz, seq, d_inner), BF16),
        scratch_shapes=[
            pltpu.VMEM((tc + 2 * SUBLANE, d_conv), F32),
            pltpu.VMEM((tc, d_conv), F32),
            pltpu.VMEM((D_STATE, d_inner), F32),
            pltpu.VMEM((CHUNK, LANE), F32),
            pltpu.VMEM((LANE, CHUNK), F32),
            pltpu.VMEM((LANE, CHUNK), F32),
            pltpu.VMEM((CHUNK, LANE), F32),
            pltpu.VMEM((CHUNK, d_inner), F32),
        ],
        compiler_params=pltpu.CompilerParams(
            dimension_semantics=("parallel", "arbitrary"), vmem_limit_bytes=VMEM_LIMIT),
        name="ssd",
    )(proj, proj, proj, dt_raw, conv_w, conv_b, dt_bias_row, a_row, d_exp, norm_g, tril2, expand2)


def _mix_kernel(x_ref, mod_ref, y_ref, scb_ref, scc_ref, sch_ref, gs_ref, gc_ref,
                wso_ref, scw_ref, wsc_ref, wo_ref, pg_ref, o_ref, v_ref, *, tm):
    s_idx = pl.program_id(1)

    @pl.when(s_idx == 0)
    def _():
        v_ref[0:SUBLANE, :] = jnp.zeros((SUBLANE, v_ref.shape[1]), F32)

    @pl.when(s_idx != 0)
    def _():
        v_ref[0:SUBLANE, :] = v_ref[tm:tm + SUBLANE, :]

    v_ref[SUBLANE:SUBLANE + tm, :] = scc_ref[...].astype(F32) * sch_ref[...].astype(F32)
    conv = None
    for k in range(SC_CONV_K):
        off = SUBLANE - (SC_CONV_K - 1) + k
        term = v_ref[off:off + tm, :] * scw_ref[k:k + 1, :]
        conv = term if conv is None else conv + term
    u = (scb_ref[...].astype(F32) * conv).astype(BF16)
    y_sc = _dot(u, wsc_ref[...])
    y_ssd = _dot(y_ref[...], wso_ref[...])
    merged = (_sigmoid(gs_ref[...].astype(F32)) * y_ssd + _sigmoid(gc_ref[...].astype(F32)) * y_sc)
    mix = _dot(merged.astype(BF16), wo_ref[...])
    o_ref[...] = x_ref[...] + mod_ref[2:3, :] * (_rms(mix) * pg_ref[...])


def _mix_call(x, mod3, y, proj, w_ssd_out, sc_conv_w, w_sc_out, w_o, post_g, tm):
    bsz, seq, d = x.shape
    d_inner = y.shape[2]
    kern = functools.partial(_mix_kernel, tm=tm)
    const = lambda b, s: (0, 0)
    base = (2 * d_inner + 2 * N_GROUPS * D_STATE) // d

    def pblock(k):
        return pl.BlockSpec((None, tm, d), lambda b, s, k=k: (b, s, base + k))

    return pl.pallas_call(
        kern,
        grid=(bsz, seq // tm),
        in_specs=[
            pl.BlockSpec((None, tm, d), lambda b, s: (b, s, 0)),
            pl.BlockSpec((None, 6, d), lambda b, s: (b, 0, 0)),
            pl.BlockSpec((None, tm, d_inner), lambda b, s: (b, s, 0)),
            pblock(0), pblock(1), pblock(2), pblock(3), pblock(4),
            pl.BlockSpec((d_inner, d), const),
            pl.BlockSpec((SC_CONV_K, d), const),
            pl.BlockSpec((d, d), const),
            pl.BlockSpec((d, d), const),
            pl.BlockSpec((1, d), const),
        ],
        out_specs=pl.BlockSpec((None, tm, d), lambda b, s: (b, s, 0)),
        out_shape=jax.ShapeDtypeStruct((bsz, seq, d), F32),
        scratch_shapes=[pltpu.VMEM((tm + 2 * SUBLANE, d), F32)],
        compiler_params=pltpu.CompilerParams(
            dimension_semantics=("parallel", "arbitrary"), vmem_limit_bytes=VMEM_LIMIT),
        name="mix",
    )(x, mod3, y, proj, proj, proj, proj, proj, w_ssd_out, sc_conv_w, w_sc_out, w_o, post_g)


def _ffn_kernel(x_ref, mod_ref, pre_ref, wup_ref, cw_ref, cb_ref, wdn_ref, pg_ref, o_ref,
                ug_ref, uv_ref, halo_ref, *, tm, d_ff, cw):
    s_idx = pl.program_id(1)

    @pl.when(s_idx == 0)
    def _():
        halo_ref[...] = jnp.zeros_like(halo_ref)

    x = x_ref[...]
    h = (_rms(x) * pre_ref[...] * (1.0 + mod_ref[4:5, :]) + mod_ref[3:4, :]).astype(BF16)

    def conv_cols(buf_ref, c0):
        buf_ref[0:SUBLANE, :] = halo_ref[:, c0:c0 + cw]
        pre = _dot(h, wup_ref[:, c0:c0 + cw])
        buf_ref[SUBLANE:SUBLANE + tm, :] = pre
        halo_ref[:, c0:c0 + cw] = pre[tm - SUBLANE:tm, :]
        acc = cb_ref[:, c0:c0 + cw]
        for k in range(FFN_CONV_K):
            off = SUBLANE - (FFN_CONV_K - 1) + k
            acc = acc + buf_ref[off:off + tm, :] * cw_ref[k:k + 1, c0:c0 + cw]
        return acc

    f = None
    for c0 in range(0, d_ff, cw):
        gate = conv_cols(ug_ref, c0)
        val = conv_cols(uv_ref, d_ff + c0)
        act = (_silu(gate) * val).astype(BF16)
        part = _dot(act, wdn_ref[c0:c0 + cw, :])
        f = part if f is None else f + part
    o_ref[...] = x + mod_ref[5:6, :] * (_rms(f) * pg_ref[...])


def _ffn_call(x, mod3, pre_g, w_up, conv_w, conv_b, w_down, post_g, tm, cw):
    bsz, seq, d = x.shape
    d_ff = w_down.shape[0]
    kern = functools.partial(_ffn_kernel, tm=tm, d_ff=d_ff, cw=cw)
    const = lambda b, s: (0, 0)
    return pl.pallas_call(
        kern,
        grid=(bsz, seq // tm),
        in_specs=[
            pl.BlockSpec((None, tm, d), lambda b, s: (b, s, 0)),
            pl.BlockSpec((None, 6, d), lambda b, s: (b, 0, 0)),
            pl.BlockSpec((1, d), const),
            pl.BlockSpec((d, 2 * d_ff), const, pipeline_mode=pl.Buffered(1)),
            pl.BlockSpec((FFN_CONV_K, 2 * d_ff), const),
            pl.BlockSpec((1, 2 * d_ff), const),
            pl.BlockSpec((d_ff, d), const, pipeline_mode=pl.Buffered(1)),
            pl.BlockSpec((1, d), const),
        ],
        out_specs=pl.BlockSpec((None, tm, d), lambda b, s: (b, s, 0)),
        out_shape=jax.ShapeDtypeStruct((bsz, seq, d), F32),
        scratch_shapes=[
            pltpu.VMEM((tm + SUBLANE, cw), F32),
            pltpu.VMEM((tm + SUBLANE, cw), F32),
            pltpu.VMEM((SUBLANE, 2 * d_ff), F32),
        ],
        compiler_params=pltpu.CompilerParams(
            dimension_semantics=("parallel", "arbitrary"), vmem_limit_bytes=VMEM_LIMIT),
        name="ffn",
    )(x, mod3, pre_g, w_up, conv_w, conv_b, w_down, post_g)


def _pick_tile(n, pref):
    t = min(n, pref)
    while n % t:
        t //= 2
    return t


def kernel(x, c, ada_w, ada_b, mix_pre_g, mix_post_g, w_in, ssd_conv_w, ssd_conv_b, ssd_dt_bias, ssd_a_log,
           ssd_d, ssd_norm_g, w_ssd_out, sc_conv_w, w_sc_out, w_o, ffn_pre_g, ffn_post_g, w_up, ffn_conv_w,
           ffn_conv_b, w_down):
    bsz, seq, d = x.shape
    depth = ada_w.shape[0]
    d_inner = ssd_norm_g.shape[1]
    n_heads = ssd_dt_bias.shape[1]
    d_conv = ssd_conv_w.shape[2]
    d_ff = w_down.shape[1]
    assert d_inner == n_heads * HEAD_DIM and d_conv == d_inner + 2 * N_GROUPS * D_STATE
    assert n_heads <= LANE and seq % CHUNK == 0

    mod = _ada_call(c, ada_w, ada_b).reshape(depth, bsz, 6, d)

    tril = jnp.tril(jnp.ones((CHUNK, CHUNK), BF16))
    tril2 = jnp.concatenate([tril, tril], axis=1)
    head_of_lane = jnp.arange(d_inner) // HEAD_DIM
    expand = (jnp.arange(LANE)[:, None] == head_of_lane[None, :]).astype(BF16)
    expand2 = jnp.concatenate([expand, expand], axis=0)

    tm_in = _pick_tile(seq, 1024)
    tc = _pick_tile(seq, 512)
    tm = _pick_tile(seq, 512)

    for i in range(depth):
        wi = w_in[i]
        o_dt = d_inner + d_conv
        o_sc = o_dt + n_heads
        w_main = jnp.concatenate([wi[:, :o_dt], wi[:, o_sc:]], axis=1).astype(BF16)
        w_dt = jnp.pad(wi[:, o_dt:o_sc], ((0, 0), (0, LANE - n_heads))).astype(BF16)
        pad_h = (0, LANE - n_heads)
        dt_bias_row = jnp.pad(ssd_dt_bias[i], pad_h).reshape(1, LANE)
        a_row = jnp.pad(-jnp.exp(ssd_a_log[i]), pad_h).reshape(1, LANE)
        d_exp = jnp.repeat(ssd_d[i], HEAD_DIM).reshape(1, d_inner)

        proj, dt_raw = _inproj_call(x, mod[i], mix_pre_g[i].reshape(1, d), w_main, w_dt, tm_in, 1024)
        y = _ssd_call(proj, dt_raw, ssd_conv_w[i], ssd_conv_b[i].reshape(1, d_conv), dt_bias_row, a_row, d_exp,
                      ssd_norm_g[i].reshape(1, d_inner), tril2, expand2, tc, d_inner)
        x = _mix_call(x, mod[i], y, proj, w_ssd_out[i].astype(BF16), sc_conv_w[i], w_sc_out[i].astype(BF16),
                      w_o[i].astype(BF16), mix_post_g[i].reshape(1, d), tm)
        x = _ffn_call(x, mod[i], ffn_pre_g[i].reshape(1, d), w_up[i].astype(BF16), ffn_conv_w[i],
                      ffn_conv_b[i].reshape(1, 2 * d_ff), w_down[i].astype(BF16), ffn_post_g[i].reshape(1, d),
                      tm, 256)
    return x
```

```python
import functools
import math

import jax
import jax.numpy as jnp
from jax import lax
from jax.experimental import pallas as pl
from jax.experimental.pallas import tpu as pltpu

F32 = jnp.float32
BF16 = jnp.bfloat16

EPS = 1e-6
LANE = 128
SUBLANE = 8
VMEM_LIMIT = 56 * 1024 * 1024

HEAD_DIM = 64
N_GROUPS = 4
D_STATE = 128
BLOCK = 128
SEG = BLOCK // SUBLANE
SSD_CONV_K = 4
SC_CONV_K = 3
FFN_CONV_K = 3
LOG2E = math.log2(math.e)


def _dot(a, b):
    return jnp.dot(a, b, preferred_element_type=F32)


def _sigmoid(v):
    return 1.0 / (1.0 + jnp.exp(-v))


def _silu(v):
    return v * _sigmoid(v)


def _split_bf16(v):
    hi = v.astype(BF16)
    lo = (v - hi.astype(F32)).astype(BF16)
    return hi, lo


def _rms(v):
    return v * lax.rsqrt(jnp.mean(v * v, axis=-1, keepdims=True) + EPS)


def _token_of_row(r):
    return (r & (SUBLANE - 1)) * SEG + (r >> 3)


def _block_shifts(cur, prev_tail, kmax):
    n = cur.shape[0]
    tail = cur[n - kmax * SUBLANE:n]
    first_sublane = lax.broadcasted_iota(jnp.int32, (SUBLANE, cur.shape[1]), 0) == 0
    wrapped = []
    for m in range(kmax):
        rows = slice(m * SUBLANE, (m + 1) * SUBLANE)
        wrapped.append(jnp.where(first_sublane, pltpu.roll(prev_tail[rows], 1, 0), pltpu.roll(tail[rows], 1, 0)))
    out = [cur]
    for k in range(1, kmax + 1):
        out.append(jnp.concatenate(wrapped[kmax - k:] + [cur[0:n - k * SUBLANE]], axis=0))
    return out


def _causal_conv(val, halo, w_ref, c0, width, kw):
    kmax = kw - 1
    outs = []
    for b0 in range(0, val.shape[0], BLOCK):
        cur = val[b0:b0 + BLOCK]
        prev_tail = halo if b0 == 0 else val[b0 - kmax * SUBLANE:b0]
        sh = _block_shifts(cur, prev_tail, kmax)
        acc = sh[0] * w_ref[kmax:kmax + 1, c0:c0 + width]
        for k in range(1, kw):
            acc = acc + sh[k] * w_ref[kmax - k:kmax - k + 1, c0:c0 + width]
        outs.append(acc)
    return outs[0] if len(outs) == 1 else jnp.concatenate(outs, axis=0)


def _ada_kernel(c_ref, w_ref, b_ref, o_ref):
    ca = _silu(c_ref[...])
    c_hi, c_lo = _split_bf16(ca)
    w_hi, w_lo = _split_bf16(w_ref[...])
    acc = _dot(c_hi, w_hi) + _dot(c_hi, w_lo) + _dot(c_lo, w_hi)
    o_ref[...] = acc + b_ref[...]


def _ada_call(c, ada_w, ada_b):
    depth, d, n = ada_w.shape
    bsz = c.shape[0]
    tn = 1024
    return pl.pallas_call(
        _ada_kernel,
        grid=(depth, n // tn),
        in_specs=[
            pl.BlockSpec((bsz, d), lambda l, j: (0, 0)),
            pl.BlockSpec((None, d, tn), lambda l, j: (l, 0, j)),
            pl.BlockSpec((None, 1, tn), lambda l, j: (l, 0, j)),
        ],
        out_specs=pl.BlockSpec((None, bsz, tn), lambda l, j: (l, 0, j)),
        out_shape=jax.ShapeDtypeStruct((depth, bsz, n), F32),
        compiler_params=pltpu.CompilerParams(
            dimension_semantics=("arbitrary", "arbitrary"), vmem_limit_bytes=VMEM_LIMIT),
        name="ada_mod",
    )(c, ada_w, ada_b.reshape(depth, 1, n))


def _inproj_kernel(x_ref, mod_ref, g_ref, w_ref, wdt_ref, o_ref, dt_ref, h_ref):
    j = pl.program_id(2)

    @pl.when(j == 0)
    def _():
        h = _rms(x_ref[...]) * g_ref[...] * (1.0 + mod_ref[1:2, :]) + mod_ref[0:1, :]
        hb = h.astype(BF16)
        h_ref[...] = hb
        dt_ref[...] = _dot(hb, wdt_ref[...])

    o_ref[...] = _dot(h_ref[...], w_ref[...]).astype(BF16)


def _inproj_call(x, mod3, g, w_main, w_dt, tm, tn):
    bsz, seq, d = x.shape
    n = w_main.shape[1]
    return pl.pallas_call(
        _inproj_kernel,
        grid=(bsz, seq // tm, n // tn),
        in_specs=[
            pl.BlockSpec((None, tm, d), lambda b, i, j: (b, i, 0)),
            pl.BlockSpec((None, 6, d), lambda b, i, j: (b, 0, 0)),
            pl.BlockSpec((1, d), lambda b, i, j: (0, 0)),
            pl.BlockSpec((d, tn), lambda b, i, j: (0, j)),
            pl.BlockSpec((d, LANE), lambda b, i, j: (0, 0)),
        ],
        out_specs=[
            pl.BlockSpec((None, tm, tn), lambda b, i, j: (b, i, j)),
            pl.BlockSpec((None, tm, LANE), lambda b, i, j: (b, i, 0)),
        ],
        out_shape=[
            jax.ShapeDtypeStruct((bsz, seq, n), BF16),
            jax.ShapeDtypeStruct((bsz, seq, LANE), F32),
        ],
        scratch_shapes=[pltpu.VMEM((tm, d), BF16)],
        compiler_params=pltpu.CompilerParams(
            dimension_semantics=("parallel", "parallel", "arbitrary"), vmem_limit_bytes=VMEM_LIMIT),
        name="inproj",
    )(x, mod3, g, w_main, w_dt)


def _ssd_kernel(z_ref, xr_ref, bcr_ref, dtr_ref, cw_ref, cb_ref, dtb_ref, a_ref, dexp_ref, ng_ref,
                tril_ref, exp_ref, o_ref,
                tail_ref, xc_ref, state_ref, a2t_ref, y_ref,
                *, tc, d_inner):
    n_heads = d_inner // HEAD_DIM
    hpg = n_heads // N_GROUPS
    gw = hpg * HEAD_DIM
    d_conv = d_inner + 2 * N_GROUPS * D_STATE
    kmax = SSD_CONV_K - 1

    @pl.when(pl.program_id(1) == 0)
    def _():
        tail_ref[...] = jnp.zeros_like(tail_ref)
        state_ref[...] = jnp.zeros_like(state_ref)

    tok_r = _token_of_row(lax.broadcasted_iota(jnp.int32, (BLOCK, BLOCK), 0))
    tok_c = _token_of_row(lax.broadcasted_iota(jnp.int32, (BLOCK, BLOCK), 1))
    causal = tok_r >= tok_c
    low_half = lax.broadcasted_iota(jnp.int32, (BLOCK, LANE), 1) < HEAD_DIM
    b_off = d_inner
    c_off = d_inner + N_GROUPS * D_STATE
    slab = 512

    def chunk_body(ci, carry):
        r0 = pl.multiple_of(ci * BLOCK, BLOCK)
        rows = pl.ds(r0, BLOCK)

        for c0 in range(0, d_conv, slab):
            if c0 < d_inner:
                cur = xr_ref[rows, c0:c0 + slab].astype(F32)
            else:
                cur = bcr_ref[rows, c0 - d_inner:c0 - d_inner + slab].astype(F32)
            halo = tail_ref[:, c0:c0 + slab]
            tail_ref[:, c0:c0 + slab] = cur[BLOCK - kmax * SUBLANE:BLOCK]
            acc = _causal_conv(cur, halo, cw_ref, c0, slab, SSD_CONV_K) + cb_ref[:, c0:c0 + slab]
            xc_ref[:, c0:c0 + slab] = _silu(acc)

        dtv = dtr_ref[rows, :] + dtb_ref[...]
        dt = jnp.maximum(dtv, 0.0) + jnp.log(1.0 + jnp.exp(-jnp.abs(dtv)))
        adt_hi, adt_lo = _split_bf16(dt * a_ref[...])
        a_cs = _dot(tril_ref[...], jnp.concatenate([adt_hi, adt_lo], axis=0))
        a_last = a_cs[BLOCK - 1:BLOCK, :]
        a2 = a_cs * LOG2E
        ea = jnp.exp(a_cs)
        a2t_ref[...] = a2.T

        factors = jnp.concatenate(
            [jnp.exp(a_last - a_cs), dt, jnp.broadcast_to(jnp.exp(a_last), (2 * SUBLANE, LANE))], axis=0)
        f_hi, f_lo = _split_bf16(factors)
        fac = _dot(jnp.concatenate([f_hi, f_lo], axis=1), exp_ref[...])
        dec_exp = fac[0:BLOCK, :]
        dt_exp = fac[BLOCK:2 * BLOCK, :]
        cd_exp = fac[2 * BLOCK:2 * BLOCK + 1, :]

        for g in range(N_GROUPS):
            gsl = slice(g * gw, (g + 1) * gw)
            bg = xc_ref[:, b_off + g * D_STATE:b_off + (g + 1) * D_STATE]
            cg = xc_ref[:, c_off + g * D_STATE:c_off + (g + 1) * D_STATE]
            scores = lax.dot_general(cg.astype(BF16), bg.astype(BF16), (((1,), (1,)), ((), ())),
                                     preferred_element_type=F32)
            scores = jnp.where(causal, scores, 0.0)
            xg = xc_ref[:, gsl]
            xdt = xg * dt_exp[:, gsl]
            hg = state_ref[:, gsl]
            for j in range(hpg // 2):
                lanes = slice(j * LANE, (j + 1) * LANE)
                rhs = jnp.concatenate([xdt[:, lanes].astype(BF16), hg[:, lanes].astype(BF16)], axis=0)
                outs = []
                for q in range(2):
                    r = g * hpg + 2 * j + q
                    decay = jnp.exp2(jnp.minimum(a2[:, r:r + 1] - a2t_ref[r:r + 1, :], 0.0))
                    m = scores * decay
                    ce = cg * ea[:, r:r + 1]
                    lhs = jnp.concatenate([m.astype(BF16), ce.astype(BF16)], axis=1)
                    outs.append(_dot(lhs, rhs))
                y_pair = jnp.where(low_half, outs[0], outs[1])
                gl = slice(g * gw + j * LANE, g * gw + (j + 1) * LANE)
                y_ref[:, gl] = y_pair + dexp_ref[:, gl] * xg[:, lanes]
            xw = (xdt * dec_exp[:, gsl]).astype(BF16)
            upd = _dot(bg.T.astype(BF16), xw)
            state_ref[:, gsl] = cd_exp[:, gsl] * hg + upd

        y = y_ref[...] * _silu(z_ref[rows, :].astype(F32))
        for g in range(N_GROUPS):
            gsl = slice(g * gw, (g + 1) * gw)
            o_ref[rows, gsl] = (_rms(y[:, gsl]) * ng_ref[:, gsl]).astype(BF16)
        return carry

    lax.fori_loop(0, tc // BLOCK, chunk_body, 0)


def _ssd_call(proj, dt_raw, conv_w, conv_b, dt_bias_row, a_row, d_exp, norm_g, tril2, expand2, tc, d_inner):
    bsz, seq, _ = proj.shape
    d_bc = 2 * N_GROUPS * D_STATE
    d_conv = d_inner + d_bc
    kern = functools.partial(_ssd_kernel, tc=tc, d_inner=d_inner)
    const = lambda b, s: (0, 0)
    return pl.pallas_call(
        kern,
        grid=(bsz, seq // tc),
        in_specs=[
            pl.BlockSpec((None, tc, d_inner), lambda b, s: (b, s, 0)),
            pl.BlockSpec((None, tc, d_inner), lambda b, s: (b, s, 1)),
            pl.BlockSpec((None, tc, d_bc), lambda b, s: (b, s, 2 * d_inner // d_bc)),
            pl.BlockSpec((None, tc, LANE), lambda b, s: (b, s, 0)),
            pl.BlockSpec((SSD_CONV_K, d_conv), const),
            pl.BlockSpec((1, d_conv), const),
            pl.BlockSpec((1, LANE), const),
            pl.BlockSpec((1, LANE), const),
            pl.BlockSpec((1, d_inner), const),
            pl.BlockSpec((1, d_inner), const),
            pl.BlockSpec((BLOCK, 2 * BLOCK), const),
            pl.BlockSpec((2 * LANE, d_inner), const),
        ],
        out_specs=pl.BlockSpec((None, tc, d_inner), lambda b, s: (b, s, 0)),
        out_shape=jax.ShapeDtypeStruct((bsz, seq, d_inner), BF16),
        scratch_shapes=[
            pltpu.VMEM(((SSD_CONV_K - 1) * SUBLANE, d_conv), F32),
            pltpu.VMEM((BLOCK, d_conv), F32),
            pltpu.VMEM((D_STATE, d_inner), F32),
            pltpu.VMEM((LANE, BLOCK), F32),
            pltpu.VMEM((BLOCK, d_inner), F32),
        ],
        compiler_params=pltpu.CompilerParams(
            dimension_semantics=("parallel", "arbitrary"), vmem_limit_bytes=VMEM_LIMIT),
        name="ssd",
    )(proj, proj, proj, dt_raw, conv_w, conv_b, dt_bias_row, a_row, d_exp, norm_g, tril2, expand2)


def _mix_kernel(x_ref, mod_ref, y_ref, scb_ref, scc_ref, sch_ref, gs_ref, gc_ref,
                wso_ref, scw_ref, wsc_ref, wo_ref, pg_ref, o_ref, halo_ref, *, tm):
    d = x_ref.shape[1]

    @pl.when(pl.program_id(1) == 0)
    def _():
        halo_ref[...] = jnp.zeros_like(halo_ref)

    v = scc_ref[...].astype(F32) * sch_ref[...].astype(F32)
    halo = halo_ref[...]
    halo_ref[...] = v[tm - (SC_CONV_K - 1) * SUBLANE:tm]
    conv = _causal_conv(v, halo, scw_ref, 0, d, SC_CONV_K)
    u = (scb_ref[...].astype(F32) * conv).astype(BF16)
    y_sc = _dot(u, wsc_ref[...])
    y_ssd = _dot(y_ref[...], wso_ref[...])
    merged = (_sigmoid(gs_ref[...].astype(F32)) * y_ssd + _sigmoid(gc_ref[...].astype(F32)) * y_sc)
    mix = _dot(merged.astype(BF16), wo_ref[...])
    o_ref[...] = x_ref[...] + mod_ref[2:3, :] * (_rms(mix) * pg_ref[...])


def _mix_call(x, mod3, y, proj, w_ssd_out, sc_conv_w, w_sc_out, w_o, post_g, tm):
    bsz, seq, d = x.shape
    d_inner = y.shape[2]
    kern = functools.partial(_mix_kernel, tm=tm)
    const = lambda b, s: (0, 0)
    base = (2 * d_inner + 2 * N_GROUPS * D_STATE) // d

    def pblock(k):
        return pl.BlockSpec((None, tm, d), lambda b, s, k=k: (b, s, base + k))

    return pl.pallas_call(
        kern,
        grid=(bsz, seq // tm),
        in_specs=[
            pl.BlockSpec((None, tm, d), lambda b, s: (b, s, 0)),
            pl.BlockSpec((None, 6, d), lambda b, s: (b, 0, 0)),
            pl.BlockSpec((None, tm, d_inner), lambda b, s: (b, s, 0)),
            pblock(0), pblock(1), pblock(2), pblock(3), pblock(4),
            pl.BlockSpec((d_inner, d), const),
            pl.BlockSpec((SC_CONV_K, d), const),
            pl.BlockSpec((d, d), const),
            pl.BlockSpec((d, d), const),
            pl.BlockSpec((1, d), const),
        ],
        out_specs=pl.BlockSpec((None, tm, d), lambda b, s: (b, s, 0)),
        out_shape=jax.ShapeDtypeStruct((bsz, seq, d), F32),
        scratch_shapes=[pltpu.VMEM(((SC_CONV_K - 1) * SUBLANE, d), F32)],
        compiler_params=pltpu.CompilerParams(
            dimension_semantics=("parallel", "arbitrary"), vmem_limit_bytes=VMEM_LIMIT),
        name="mix",
    )(x, mod3, y, proj, proj, proj, proj, proj, w_ssd_out, sc_conv_w, w_sc_out, w_o, post_g)


def _ffn_kernel(x_ref, mod_ref, pre_ref, wup_ref, cw_ref, cb_ref, wdn_ref, pg_ref, o_ref,
                halo_ref, *, tm, d_ff, cw):
    @pl.when(pl.program_id(1) == 0)
    def _():
        halo_ref[...] = jnp.zeros_like(halo_ref)

    x = x_ref[...]
    h = (_rms(x) * pre_ref[...] * (1.0 + mod_ref[4:5, :]) + mod_ref[3:4, :]).astype(BF16)

    def conv_cols(c0):
        pre = _dot(h, wup_ref[:, c0:c0 + cw])
        halo = halo_ref[:, c0:c0 + cw]
        halo_ref[:, c0:c0 + cw] = pre[tm - (FFN_CONV_K - 1) * SUBLANE:tm]
        return _causal_conv(pre, halo, cw_ref, c0, cw, FFN_CONV_K) + cb_ref[:, c0:c0 + cw]

    f = None
    for c0 in range(0, d_ff, cw):
        gate = conv_cols(c0)
        val = conv_cols(d_ff + c0)
        act = (_silu(gate) * val).astype(BF16)
        part = _dot(act, wdn_ref[c0:c0 + cw, :])
        f = part if f is None else f + part
    o_ref[...] = x + mod_ref[5:6, :] * (_rms(f) * pg_ref[...])


def _ffn_call(x, mod3, pre_g, w_up, conv_w, conv_b, w_down, post_g, tm, cw):
    bsz, seq, d = x.shape
    d_ff = w_down.shape[0]
    kern = functools.partial(_ffn_kernel, tm=tm, d_ff=d_ff, cw=cw)
    const = lambda b, s: (0, 0)
    return pl.pallas_call(
        kern,
        grid=(bsz, seq // tm),
        in_specs=[
            pl.BlockSpec((None, tm, d), lambda b, s: (b, s, 0)),
            pl.BlockSpec((None, 6, d), lambda b, s: (b, 0, 0)),
            pl.BlockSpec((1, d), const),
            pl.BlockSpec((d, 2 * d_ff), const, pipeline_mode=pl.Buffered(1)),
            pl.BlockSpec((FFN_CONV_K, 2 * d_ff), const),
            pl.BlockSpec((1, 2 * d_ff), const),
            pl.BlockSpec((d_ff, d), const, pipeline_mode=pl.Buffered(1)),
            pl.BlockSpec((1, d), const),
        ],
        out_specs=pl.BlockSpec((None, tm, d), lambda b, s: (b, s, 0)),
        out_shape=jax.ShapeDtypeStruct((bsz, seq, d), F32),
        scratch_shapes=[pltpu.VMEM(((FFN_CONV_K - 1) * SUBLANE, 2 * d_ff), F32)],
        compiler_params=pltpu.CompilerParams(
            dimension_semantics=("parallel", "arbitrary"), vmem_limit_bytes=VMEM_LIMIT),
        name="ffn",
    )(x, mod3, pre_g, w_up, conv_w, conv_b, w_down, post_g)


def _pick_tile(n, pref):
    t = min(n, pref)
    while n % t:
        t //= 2
    return t


def _to_block_order(x):
    b, s, d = x.shape
    return x.reshape(b, s // BLOCK, SUBLANE, SEG, d).transpose(0, 1, 3, 2, 4).reshape(b, s, d)


def _from_block_order(x):
    b, s, d = x.shape
    return x.reshape(b, s // BLOCK, SEG, SUBLANE, d).transpose(0, 1, 3, 2, 4).reshape(b, s, d)


def kernel(x, c, ada_w, ada_b, mix_pre_g, mix_post_g, w_in, ssd_conv_w, ssd_conv_b, ssd_dt_bias, ssd_a_log,
           ssd_d, ssd_norm_g, w_ssd_out, sc_conv_w, w_sc_out, w_o, ffn_pre_g, ffn_post_g, w_up, ffn_conv_w,
           ffn_conv_b, w_down):
    bsz, seq, d = x.shape
    depth = ada_w.shape[0]
    d_inner = ssd_norm_g.shape[1]
    n_heads = ssd_dt_bias.shape[1]
    d_conv = ssd_conv_w.shape[2]
    d_ff = w_down.shape[1]
    assert d_inner == n_heads * HEAD_DIM and d_conv == d_inner + 2 * N_GROUPS * D_STATE
    assert n_heads <= LANE and seq % BLOCK == 0

    mod = _ada_call(c, ada_w, ada_b).reshape(depth, bsz, 6, d)

    tok = _token_of_row(jnp.arange(BLOCK))
    tril = (tok[:, None] >= tok[None, :]).astype(BF16)
    tril2 = jnp.concatenate([tril, tril], axis=1)
    head_of_lane = jnp.arange(d_inner) // HEAD_DIM
    expand = (jnp.arange(LANE)[:, None] == head_of_lane[None, :]).astype(BF16)
    expand2 = jnp.concatenate([expand, expand], axis=0)

    tm_in = _pick_tile(seq, 1024)
    tc = _pick_tile(seq, 512)
    tm = _pick_tile(seq, 512)

    x = _to_block_order(x)
    for i in range(depth):
        wi = w_in[i]
        o_dt = d_inner + d_conv
        o_sc = o_dt + n_heads
        w_main = jnp.concatenate([wi[:, :o_dt], wi[:, o_sc:]], axis=1).astype(BF16)
        w_dt = jnp.pad(wi[:, o_dt:o_sc], ((0, 0), (0, LANE - n_heads))).astype(BF16)
        pad_h = (0, LANE - n_heads)
        dt_bias_row = jnp.pad(ssd_dt_bias[i], pad_h).reshape(1, LANE)
        a_row = jnp.pad(-jnp.exp(ssd_a_log[i]), pad_h).reshape(1, LANE)
        d_exp = jnp.repeat(ssd_d[i], HEAD_DIM).reshape(1, d_inner)

        proj, dt_raw = _inproj_call(x, mod[i], mix_pre_g[i].reshape(1, d), w_main, w_dt, tm_in, 1024)
        y = _ssd_call(proj, dt_raw, ssd_conv_w[i], ssd_conv_b[i].reshape(1, d_conv), dt_bias_row, a_row, d_exp,
                      ssd_norm_g[i].reshape(1, d_inner), tril2, expand2, tc, d_inner)
        x = _mix_call(x, mod[i], y, proj, w_ssd_out[i].astype(BF16), sc_conv_w[i], w_sc_out[i].astype(BF16),
                      w_o[i].astype(BF16), mix_post_g[i].reshape(1, d), tm)
        x = _ffn_call(x, mod[i], ffn_pre_g[i].reshape(1, d), w_up[i].astype(BF16), ffn_conv_w[i],
                      ffn_conv_b[i].reshape(1, 2 * d_ff), w_down[i].astype(BF16), ffn_post_g[i].reshape(1, d),
                      tm, 256)
    return _from_block_order(x)
```

```python
import functools
import math

import jax
import jax.numpy as jnp
from jax import lax
from jax.experimental import pallas as pl
from jax.experimental.pallas import tpu as pltpu

F32 = jnp.float32
BF16 = jnp.bfloat16

EPS = 1e-6
LANE = 128
SUBLANE = 8
VMEM_LIMIT = 56 * 1024 * 1024

HEAD_DIM = 64
N_GROUPS = 4
D_STATE = 128
BLOCK = 128
SEG = BLOCK // SUBLANE
SSD_CONV_K = 4
SC_CONV_K = 3
FFN_CONV_K = 3
LOG2E = math.log2(math.e)


def _dot(a, b):
    return jnp.dot(a, b, preferred_element_type=F32)


def _sigmoid(v):
    return 1.0 / (1.0 + jnp.exp(-v))


def _silu(v):
    return v * _sigmoid(v)


def _silu_of_double(hv):
    return hv + hv * jnp.tanh(hv)


def _sigmoid_of_double(hv):
    return 0.5 + 0.5 * jnp.tanh(hv)


def _split_bf16(v):
    hi = v.astype(BF16)
    lo = (v - hi.astype(F32)).astype(BF16)
    return hi, lo


def _rms(v):
    return v * lax.rsqrt(jnp.mean(v * v, axis=-1, keepdims=True) + EPS)


def _token_of_row(r):
    return (r & (SUBLANE - 1)) * SEG + (r >> 3)


def _block_shifts(cur, prev_tail, kmax):
    n = cur.shape[0]
    tail = cur[n - kmax * SUBLANE:n]
    first_sublane = lax.broadcasted_iota(jnp.int32, (SUBLANE, cur.shape[1]), 0) == 0
    wrapped = []
    for m in range(kmax):
        rows = slice(m * SUBLANE, (m + 1) * SUBLANE)
        wrapped.append(jnp.where(first_sublane, pltpu.roll(prev_tail[rows], 1, 0), pltpu.roll(tail[rows], 1, 0)))
    out = [cur]
    for k in range(1, kmax + 1):
        out.append(jnp.concatenate(wrapped[kmax - k:] + [cur[0:n - k * SUBLANE]], axis=0))
    return out


def _causal_conv(val, halo, w_ref, c0, width, kw):
    kmax = kw - 1
    outs = []
    for b0 in range(0, val.shape[0], BLOCK):
        cur = val[b0:b0 + BLOCK]
        prev_tail = halo if b0 == 0 else val[b0 - kmax * SUBLANE:b0]
        sh = _block_shifts(cur, prev_tail, kmax)
        acc = sh[0] * w_ref[kmax:kmax + 1, c0:c0 + width]
        for k in range(1, kw):
            acc = acc + sh[k] * w_ref[kmax - k:kmax - k + 1, c0:c0 + width]
        outs.append(acc)
    return outs[0] if len(outs) == 1 else jnp.concatenate(outs, axis=0)


def _ada_kernel(c_ref, w_ref, b_ref, o_ref):
    ca = _silu(c_ref[...])
    c_hi, c_lo = _split_bf16(ca)
    w_hi, w_lo = _split_bf16(w_ref[...])
    acc = _dot(c_hi, w_hi) + _dot(c_hi, w_lo) + _dot(c_lo, w_hi)
    o_ref[...] = acc + b_ref[...]


def _ada_call(c, ada_w, ada_b):
    depth, d, n = ada_w.shape
    bsz = c.shape[0]
    tn = 1024
    return pl.pallas_call(
        _ada_kernel,
        grid=(depth, n // tn),
        in_specs=[
            pl.BlockSpec((bsz, d), lambda l, j: (0, 0)),
            pl.BlockSpec((None, d, tn), lambda l, j: (l, 0, j)),
            pl.BlockSpec((None, 1, tn), lambda l, j: (l, 0, j)),
        ],
        out_specs=pl.BlockSpec((None, bsz, tn), lambda l, j: (l, 0, j)),
        out_shape=jax.ShapeDtypeStruct((depth, bsz, n), F32),
        compiler_params=pltpu.CompilerParams(
            dimension_semantics=("arbitrary", "arbitrary"), vmem_limit_bytes=VMEM_LIMIT),
        name="ada_mod",
    )(c, ada_w, ada_b.reshape(depth, 1, n))


def _inproj_kernel(x_ref, mod_ref, g_ref, w_ref, wdt_ref, cw_ref, cb_ref, o_ref, dt_ref, h_ref, halo_ref,
                   *, tm, n_z, n_conv):
    i = pl.program_id(1)
    j = pl.program_id(2)
    kmax = SSD_CONV_K - 1

    @pl.when(j == 0)
    def _():
        h = _rms(x_ref[...]) * g_ref[...] * (1.0 + mod_ref[1:2, :]) + mod_ref[0:1, :]
        hb = h.astype(BF16)
        h_ref[...] = hb
        dt_ref[...] = _dot(hb, wdt_ref[...])

        @pl.when(i == 0)
        def _():
            halo_ref[...] = jnp.zeros_like(halo_ref)

    tn = w_ref.shape[1]
    sub = 256
    rh = min(tm, 512)

    @pl.when(j < n_z)
    def _():
        for c0 in range(0, tn, sub):
            for r0 in range(0, tm, rh):
                half_z = _dot(h_ref[r0:r0 + rh, :], w_ref[:, c0:c0 + sub])
                o_ref[r0:r0 + rh, c0:c0 + sub] = _silu_of_double(half_z).astype(BF16)

    @pl.when((j >= n_z) & (j < n_z + n_conv))
    def _():
        k = j - n_z
        for c0 in range(0, tn, sub):
            halo = halo_ref[k, :, c0:c0 + sub]
            half_bias = 0.5 * cb_ref[:, c0:c0 + sub]
            for r0 in range(0, tm, rh):
                pre = _dot(h_ref[r0:r0 + rh, :], w_ref[:, c0:c0 + sub])
                half_acc = _causal_conv(pre, halo, cw_ref, c0, sub, SSD_CONV_K) + half_bias
                o_ref[r0:r0 + rh, c0:c0 + sub] = _silu_of_double(half_acc).astype(BF16)
                halo = pre[rh - kmax * SUBLANE:rh]
            halo_ref[k, :, c0:c0 + sub] = halo

    @pl.when(j >= n_z + n_conv)
    def _():
        o_ref[...] = _dot(h_ref[...], w_ref[...]).astype(BF16)


def _inproj_call(x, mod3, g, w_main, w_dt, conv_w, conv_b, tm, tn, d_inner):
    bsz, seq, d = x.shape
    n = w_main.shape[1]
    d_conv = conv_w.shape[1]
    n_z, n_conv = d_inner // tn, d_conv // tn
    assert n_z * tn == d_inner and n_conv * tn == d_conv
    kern = functools.partial(_inproj_kernel, tm=tm, n_z=n_z, n_conv=n_conv)

    def conv_block(b, i, j):
        return (0, jnp.clip(j - n_z, 0, n_conv - 1))

    return pl.pallas_call(
        kern,
        grid=(bsz, seq // tm, n // tn),
        in_specs=[
            pl.BlockSpec((None, tm, d), lambda b, i, j: (b, i, 0)),
            pl.BlockSpec((None, 6, d), lambda b, i, j: (b, 0, 0)),
            pl.BlockSpec((1, d), lambda b, i, j: (0, 0)),
            pl.BlockSpec((d, tn), lambda b, i, j: (0, j)),
            pl.BlockSpec((d, LANE), lambda b, i, j: (0, 0)),
            pl.BlockSpec((SSD_CONV_K, tn), conv_block),
            pl.BlockSpec((1, tn), conv_block),
        ],
        out_specs=[
            pl.BlockSpec((None, tm, tn), lambda b, i, j: (b, i, j)),
            pl.BlockSpec((None, tm, LANE), lambda b, i, j: (b, i, 0)),
        ],
        out_shape=[
            jax.ShapeDtypeStruct((bsz, seq, n), BF16),
            jax.ShapeDtypeStruct((bsz, seq, LANE), F32),
        ],
        scratch_shapes=[
            pltpu.VMEM((tm, d), BF16),
            pltpu.VMEM((n_conv, (SSD_CONV_K - 1) * SUBLANE, tn), F32),
        ],
        compiler_params=pltpu.CompilerParams(
            dimension_semantics=("parallel", "arbitrary", "arbitrary"), vmem_limit_bytes=VMEM_LIMIT),
        name="inproj",
    )(x, mod3, g, w_main, w_dt, conv_w, conv_b)


def _ssd_kernel(zs_ref, xc_ref, bc_ref, dtr_ref, dtb_ref, a_ref, dexp_ref, ng_ref, tril_ref, exp_ref, o_ref,
                state_ref, a2t_ref, y_ref, *, tc, d_inner):
    n_heads = d_inner // HEAD_DIM
    hpg = n_heads // N_GROUPS
    gw = hpg * HEAD_DIM

    @pl.when(pl.program_id(1) == 0)
    def _():
        state_ref[...] = jnp.zeros_like(state_ref)

    tok_r = _token_of_row(lax.broadcasted_iota(jnp.int32, (BLOCK, BLOCK), 0))
    tok_c = _token_of_row(lax.broadcasted_iota(jnp.int32, (BLOCK, BLOCK), 1))
    causal = tok_r >= tok_c
    low_half = lax.broadcasted_iota(jnp.int32, (BLOCK, LANE), 1) < HEAD_DIM
    c_off = N_GROUPS * D_STATE

    def chunk_body(ci, carry):
        r0 = pl.multiple_of(ci * BLOCK, BLOCK)
        rows = pl.ds(r0, BLOCK)

        dtv = dtr_ref[rows, :] + dtb_ref[...]
        dt = jnp.maximum(dtv, 0.0) + jnp.log(1.0 + jnp.exp(-jnp.abs(dtv)))
        adt_hi, adt_lo = _split_bf16(dt * a_ref[...])
        a_cs = _dot(tril_ref[...], jnp.concatenate([adt_hi, adt_lo], axis=0))
        a_last = a_cs[BLOCK - 1:BLOCK, :]
        a2 = a_cs * LOG2E
        a2t_ref[...] = a2.T

        factors = jnp.concatenate(
            [jnp.exp(a_last - a_cs), dt, jnp.broadcast_to(jnp.exp(a_last), (2 * SUBLANE, LANE))], axis=0)
        f_hi, f_lo = _split_bf16(factors)
        fac = _dot(jnp.concatenate([f_hi, f_lo], axis=1), exp_ref[...])
        dec_exp = fac[0:BLOCK, :]
        dt_exp = fac[BLOCK:2 * BLOCK, :]
        cd_exp = fac[2 * BLOCK:2 * BLOCK + 1, :]

        for g in range(N_GROUPS):
            gsl = slice(g * gw, (g + 1) * gw)
            bg = bc_ref[rows, g * D_STATE:(g + 1) * D_STATE]
            cgb = bc_ref[rows, c_off + g * D_STATE:c_off + (g + 1) * D_STATE]
            cg = cgb.astype(F32)
            scores = lax.dot_general(cgb, bg, (((1,), (1,)), ((), ())), preferred_element_type=F32)
            scores = jnp.where(causal, scores, 0.0)
            xg = xc_ref[rows, gsl].astype(F32)
            xdt = xg * dt_exp[:, gsl]
            hg = state_ref[:, gsl]
            for j in range(hpg // 2):
                lanes = slice(j * LANE, (j + 1) * LANE)
                rhs = jnp.concatenate([xdt[:, lanes].astype(BF16), hg[:, lanes].astype(BF16)], axis=0)
                outs = []
                for q in range(2):
                    r = g * hpg + 2 * j + q
                    a2_col = jnp.broadcast_to(a2[:, r:r + 1], (BLOCK, BLOCK))
                    decay = jnp.exp2(jnp.minimum(a2_col - a2t_ref[r:r + 1, :], 0.0))
                    m = scores * decay
                    ce = cg * jnp.exp2(a2_col)
                    lhs = jnp.concatenate([m.astype(BF16), ce.astype(BF16)], axis=1)
                    outs.append(_dot(lhs, rhs))
                y_pair = jnp.where(low_half, outs[0], outs[1])
                gl = slice(g * gw + j * LANE, g * gw + (j + 1) * LANE)
                y_ref[:, gl] = y_pair + dexp_ref[:, gl] * xg[:, lanes]
            xw = (xdt * dec_exp[:, gsl]).astype(BF16)
            upd = _dot(bg.astype(F32).T.astype(BF16), xw)
            state_ref[:, gsl] = cd_exp[:, gsl] * hg + upd

        y = y_ref[...] * zs_ref[rows, :].astype(F32)
        for g in range(N_GROUPS):
            gsl = slice(g * gw, (g + 1) * gw)
            o_ref[rows, gsl] = (_rms(y[:, gsl]) * ng_ref[:, gsl]).astype(BF16)
        return carry

    lax.fori_loop(0, tc // BLOCK, chunk_body, 0)


def _ssd_call(proj, dt_raw, dt_bias_row, a_row, d_exp, norm_g, tril2, expand2, tc, d_inner):
    bsz, seq, _ = proj.shape
    d_bc = 2 * N_GROUPS * D_STATE
    kern = functools.partial(_ssd_kernel, tc=tc, d_inner=d_inner)
    const = lambda b, s: (0, 0)
    return pl.pallas_call(
        kern,
        grid=(bsz, seq // tc),
        in_specs=[
            pl.BlockSpec((None, tc, d_inner), lambda b, s: (b, s, 0)),
            pl.BlockSpec((None, tc, d_inner), lambda b, s: (b, s, 1)),
            pl.BlockSpec((None, tc, d_bc), lambda b, s: (b, s, 2 * d_inner // d_bc)),
            pl.BlockSpec((None, tc, LANE), lambda b, s: (b, s, 0)),
            pl.BlockSpec((1, LANE), const),
            pl.BlockSpec((1, LANE), const),
            pl.BlockSpec((1, d_inner), const),
            pl.BlockSpec((1, d_inner), const),
            pl.BlockSpec((BLOCK, 2 * BLOCK), const),
            pl.BlockSpec((2 * LANE, d_inner), const),
        ],
        out_specs=pl.BlockSpec((None, tc, d_inner), lambda b, s: (b, s, 0)),
        out_shape=jax.ShapeDtypeStruct((bsz, seq, d_inner), BF16),
        scratch_shapes=[
            pltpu.VMEM((D_STATE, d_inner), F32),
            pltpu.VMEM((LANE, BLOCK), F32),
            pltpu.VMEM((BLOCK, d_inner), F32),
        ],
        compiler_params=pltpu.CompilerParams(
            dimension_semantics=("parallel", "arbitrary"), vmem_limit_bytes=VMEM_LIMIT),
        name="ssd",
    )(proj, proj, proj, dt_raw, dt_bias_row, a_row, d_exp, norm_g, tril2, expand2)


def _mix_kernel(x_ref, mod_ref, y_ref, scb_ref, scc_ref, sch_ref, gs_ref, gc_ref,
                wso_ref, scw_ref, wsc_ref, wo_ref, pg_ref, o_ref, halo_ref, *, tm):
    d = x_ref.shape[1]

    @pl.when(pl.program_id(1) == 0)
    def _():
        halo_ref[...] = jnp.zeros_like(halo_ref)

    v = scc_ref[...].astype(F32) * sch_ref[...].astype(F32)
    halo = halo_ref[...]
    halo_ref[...] = v[tm - (SC_CONV_K - 1) * SUBLANE:tm]
    conv = _causal_conv(v, halo, scw_ref, 0, d, SC_CONV_K)
    u = (scb_ref[...].astype(F32) * conv).astype(BF16)
    y_sc = _dot(u, wsc_ref[...])
    y_ssd = _dot(y_ref[...], wso_ref[...])
    merged = (_sigmoid_of_double(gs_ref[...].astype(F32)) * y_ssd
              + _sigmoid_of_double(gc_ref[...].astype(F32)) * y_sc)
    mix = _dot(merged.astype(BF16), wo_ref[...])
    o_ref[...] = x_ref[...] + mod_ref[2:3, :] * (_rms(mix) * pg_ref[...])


def _mix_call(x, mod3, y, proj, w_ssd_out, sc_conv_w, w_sc_out, w_o, post_g, tm):
    bsz, seq, d = x.shape
    d_inner = y.shape[2]
    kern = functools.partial(_mix_kernel, tm=tm)
    const = lambda b, s: (0, 0)
    base = (2 * d_inner + 2 * N_GROUPS * D_STATE) // d

    def pblock(k):
        return pl.BlockSpec((None, tm, d), lambda b, s, k=k: (b, s, base + k))

    return pl.pallas_call(
        kern,
        grid=(bsz, seq // tm),
        in_specs=[
            pl.BlockSpec((None, tm, d), lambda b, s: (b, s, 0)),
            pl.BlockSpec((None, 6, d), lambda b, s: (b, 0, 0)),
            pl.BlockSpec((None, tm, d_inner), lambda b, s: (b, s, 0)),
            pblock(0), pblock(1), pblock(2), pblock(3), pblock(4),
            pl.BlockSpec((d_inner, d), const),
            pl.BlockSpec((SC_CONV_K, d), const),
            pl.BlockSpec((d, d), const),
            pl.BlockSpec((d, d), const),
            pl.BlockSpec((1, d), const),
        ],
        out_specs=pl.BlockSpec((None, tm, d), lambda b, s: (b, s, 0)),
        out_shape=jax.ShapeDtypeStruct((bsz, seq, d), F32),
        scratch_shapes=[pltpu.VMEM(((SC_CONV_K - 1) * SUBLANE, d), F32)],
        compiler_params=pltpu.CompilerParams(
            dimension_semantics=("parallel", "arbitrary"), vmem_limit_bytes=VMEM_LIMIT),
        name="mix",
    )(x, mod3, y, proj, proj, proj, proj, proj, w_ssd_out, sc_conv_w, w_sc_out, w_o, post_g)


def _ffn_kernel(x_ref, mod_ref, pre_ref, wup_ref, cw_ref, cb_ref, wdn_ref, pg_ref, o_ref,
                halo_ref, act_ref, *, tm, d_ff, cw):
    @pl.when(pl.program_id(1) == 0)
    def _():
        halo_ref[...] = jnp.zeros_like(halo_ref)

    x = x_ref[...]
    h = (_rms(x) * pre_ref[...] * (1.0 + mod_ref[4:5, :]) + mod_ref[3:4, :]).astype(BF16)

    def conv_cols(c0, bias_scale):
        pre = _dot(h, wup_ref[:, c0:c0 + cw])
        halo = halo_ref[:, c0:c0 + cw]
        halo_ref[:, c0:c0 + cw] = pre[tm - (FFN_CONV_K - 1) * SUBLANE:tm]
        bias = cb_ref[:, c0:c0 + cw]
        if bias_scale != 1.0:
            bias = bias_scale * bias
        return _causal_conv(pre, halo, cw_ref, c0, cw, FFN_CONV_K) + bias

    for c0 in range(0, d_ff, cw):
        half_gate = conv_cols(c0, 0.5)
        val = conv_cols(d_ff + c0, 1.0)
        act_ref[:, c0:c0 + cw] = (_silu_of_double(half_gate) * val).astype(BF16)
    f = _dot(act_ref[...], wdn_ref[...])
    o_ref[...] = x + mod_ref[5:6, :] * (_rms(f) * pg_ref[...])


def _ffn_call(x, mod3, pre_g, w_up, conv_w, conv_b, w_down, post_g, tm, cw):
    bsz, seq, d = x.shape
    d_ff = w_down.shape[0]
    kern = functools.partial(_ffn_kernel, tm=tm, d_ff=d_ff, cw=cw)
    const = lambda b, s: (0, 0)
    return pl.pallas_call(
        kern,
        grid=(bsz, seq // tm),
        in_specs=[
            pl.BlockSpec((None, tm, d), lambda b, s: (b, s, 0)),
            pl.BlockSpec((None, 6, d), lambda b, s: (b, 0, 0)),
            pl.BlockSpec((1, d), const),
            pl.BlockSpec((d, 2 * d_ff), const, pipeline_mode=pl.Buffered(1)),
            pl.BlockSpec((FFN_CONV_K, 2 * d_ff), const),
            pl.BlockSpec((1, 2 * d_ff), const),
            pl.BlockSpec((d_ff, d), const, pipeline_mode=pl.Buffered(1)),
            pl.BlockSpec((1, d), const),
        ],
        out_specs=pl.BlockSpec((None, tm, d), lambda b, s: (b, s, 0)),
        out_shape=jax.ShapeDtypeStruct((bsz, seq, d), F32),
        scratch_shapes=[
            pltpu.VMEM(((FFN_CONV_K - 1) * SUBLANE, 2 * d_ff), F32),
            pltpu.VMEM((tm, d_ff), BF16),
        ],
        compiler_params=pltpu.CompilerParams(
            dimension_semantics=("parallel", "arbitrary"), vmem_limit_bytes=VMEM_LIMIT),
        name="ffn",
    )(x, mod3, pre_g, w_up, conv_w, conv_b, w_down, post_g)


def _pick_tile(n, pref):
    t = min(n, pref)
    while n % t:
        t //= 2
    return t


def _to_block_order(x):
    b, s, d = x.shape
    return x.reshape(b, s // BLOCK, SUBLANE, SEG, d).transpose(0, 1, 3, 2, 4).reshape(b, s, d)


def _from_block_order(x):
    b, s, d = x.shape
    return x.reshape(b, s // BLOCK, SEG, SUBLANE, d).transpose(0, 1, 3, 2, 4).reshape(b, s, d)


def kernel(x, c, ada_w, ada_b, mix_pre_g, mix_post_g, w_in, ssd_conv_w, ssd_conv_b, ssd_dt_bias, ssd_a_log,
           ssd_d, ssd_norm_g, w_ssd_out, sc_conv_w, w_sc_out, w_o, ffn_pre_g, ffn_post_g, w_up, ffn_conv_w,
           ffn_conv_b, w_down):
    bsz, seq, d = x.shape
    depth = ada_w.shape[0]
    d_inner = ssd_norm_g.shape[1]
    n_heads = ssd_dt_bias.shape[1]
    d_conv = ssd_conv_w.shape[2]
    d_ff = w_down.shape[1]
    assert d_inner == n_heads * HEAD_DIM and d_conv == d_inner + 2 * N_GROUPS * D_STATE
    assert n_heads <= LANE and seq % BLOCK == 0

    mod = _ada_call(c, ada_w, ada_b).reshape(depth, bsz, 6, d)

    tok = _token_of_row(jnp.arange(BLOCK))
    tril = (tok[:, None] >= tok[None, :]).astype(BF16)
    tril2 = jnp.concatenate([tril, tril], axis=1)
    head_of_lane = jnp.arange(d_inner) // HEAD_DIM
    expand = (jnp.arange(LANE)[:, None] == head_of_lane[None, :]).astype(BF16)
    expand2 = jnp.concatenate([expand, expand], axis=0)

    tm_in = _pick_tile(seq, 1024)
    tc = _pick_tile(seq, 512)
    tm = _pick_tile(seq, 512)

    x = _to_block_order(x)
    for i in range(depth):
        wi = w_in[i]
        o_dt = d_inner + d_conv
        o_sc = o_dt + n_heads
        o_gate = wi.shape[1] - 2 * d
        w_main = jnp.concatenate([0.5 * wi[:, :o_dt], wi[:, o_sc:o_gate], 0.5 * wi[:, o_gate:]],
                                 axis=1).astype(BF16)
        w_up_scaled = jnp.concatenate([0.5 * w_up[i][:, :d_ff], w_up[i][:, d_ff:]], axis=1).astype(BF16)
        w_dt = jnp.pad(wi[:, o_dt:o_sc], ((0, 0), (0, LANE - n_heads))).astype(BF16)
        pad_h = (0, LANE - n_heads)
        dt_bias_row = jnp.pad(ssd_dt_bias[i], pad_h).reshape(1, LANE)
        a_row = jnp.pad(-jnp.exp(ssd_a_log[i]), pad_h).reshape(1, LANE)
        d_exp = jnp.repeat(ssd_d[i], HEAD_DIM).reshape(1, d_inner)

        proj, dt_raw = _inproj_call(x, mod[i], mix_pre_g[i].reshape(1, d), w_main, w_dt, ssd_conv_w[i],
                                    ssd_conv_b[i].reshape(1, d_conv), tm_in, 1024, d_inner)
        y = _ssd_call(proj, dt_raw, dt_bias_row, a_row, d_exp, ssd_norm_g[i].reshape(1, d_inner), tril2, expand2,
                      tc, d_inner)
        x = _mix_call(x, mod[i], y, proj, w_ssd_out[i].astype(BF16), sc_conv_w[i], w_sc_out[i].astype(BF16),
                      w_o[i].astype(BF16), mix_post_g[i].reshape(1, d), tm)
        x = _ffn_call(x, mod[i], ffn_pre_g[i].reshape(1, d), w_up_scaled, ffn_conv_w[i],
                      ffn_conv_b[i].reshape(1, 2 * d_ff), w_down[i].astype(BF16), ffn_post_g[i].reshape(1, d),
                      tm, 256)
    return _from_block_order(x)
```

```python
import functools
import math

import jax
import jax.numpy as jnp
from jax import lax
from jax.experimental import pallas as pl
from jax.experimental.pallas import tpu as pltpu

F32 = jnp.float32
BF16 = jnp.bfloat16

EPS = 1e-6
LANE = 128
SUBLANE = 8
VMEM_LIMIT = 56 * 1024 * 1024

HEAD_DIM = 64
N_GROUPS = 4
D_STATE = 128
BLOCK = 128
SEG = BLOCK // SUBLANE
SSD_CONV_K = 4
SC_CONV_K = 3
FFN_CONV_K = 3
LOG2E = math.log2(math.e)


def _dot(a, b):
    return jnp.dot(a, b, preferred_element_type=F32)


def _sigmoid(v):
    return 1.0 / (1.0 + jnp.exp(-v))


def _silu(v):
    return v * _sigmoid(v)


def _silu_of_double(hv):
    return hv + hv * jnp.tanh(hv)


def _sigmoid_of_double(hv):
    return 0.5 + 0.5 * jnp.tanh(hv)


def _split_bf16(v):
    hi = v.astype(BF16)
    lo = (v - hi.astype(F32)).astype(BF16)
    return hi, lo


def _rms(v):
    return v * lax.rsqrt(jnp.mean(v * v, axis=-1, keepdims=True) + EPS)


def _token_of_row(r):
    return (r & (SUBLANE - 1)) * SEG + (r >> 3)


def _block_shifts(cur, prev_tail, kmax):
    n = cur.shape[0]
    tail = cur[n - kmax * SUBLANE:n]
    first_sublane = lax.broadcasted_iota(jnp.int32, (SUBLANE, cur.shape[1]), 0) == 0
    wrapped = []
    for m in range(kmax):
        rows = slice(m * SUBLANE, (m + 1) * SUBLANE)
        wrapped.append(jnp.where(first_sublane, pltpu.roll(prev_tail[rows], 1, 0), pltpu.roll(tail[rows], 1, 0)))
    out = [cur]
    for k in range(1, kmax + 1):
        out.append(jnp.concatenate(wrapped[kmax - k:] + [cur[0:n - k * SUBLANE]], axis=0))
    return out


def _causal_conv(val, halo, w_ref, c0, width, kw):
    kmax = kw - 1
    outs = []
    for b0 in range(0, val.shape[0], BLOCK):
        cur = val[b0:b0 + BLOCK]
        prev_tail = halo if b0 == 0 else val[b0 - kmax * SUBLANE:b0]
        sh = _block_shifts(cur, prev_tail, kmax)
        acc = sh[0] * w_ref[kmax:kmax + 1, c0:c0 + width]
        for k in range(1, kw):
            acc = acc + sh[k] * w_ref[kmax - k:kmax - k + 1, c0:c0 + width]
        outs.append(acc)
    return outs[0] if len(outs) == 1 else jnp.concatenate(outs, axis=0)


def _ada_kernel(c_ref, w_ref, b_ref, o_ref):
    ca = _silu(c_ref[...])
    c_hi, c_lo = _split_bf16(ca)
    w_hi, w_lo = _split_bf16(w_ref[...])
    acc = _dot(c_hi, w_hi) + _dot(c_hi, w_lo) + _dot(c_lo, w_hi)
    o_ref[...] = acc + b_ref[...]


def _ada_call(c, ada_w, ada_b):
    depth, d, n = ada_w.shape
    bsz = c.shape[0]
    tn = 1024
    return pl.pallas_call(
        _ada_kernel,
        grid=(depth, n // tn),
        in_specs=[
            pl.BlockSpec((bsz, d), lambda l, j: (0, 0)),
            pl.BlockSpec((None, d, tn), lambda l, j: (l, 0, j)),
            pl.BlockSpec((None, 1, tn), lambda l, j: (l, 0, j)),
        ],
        out_specs=pl.BlockSpec((None, bsz, tn), lambda l, j: (l, 0, j)),
        out_shape=jax.ShapeDtypeStruct((depth, bsz, n), F32),
        compiler_params=pltpu.CompilerParams(
            dimension_semantics=("arbitrary", "arbitrary"), vmem_limit_bytes=VMEM_LIMIT),
        name="ada_mod",
    )(c, ada_w, ada_b.reshape(depth, 1, n))


def _inproj_kernel(x_ref, mod_ref, g_ref, w_ref, wdt_ref, cw_ref, cb_ref, o_ref, dt_ref, halo_ref,
                   *, tm, d_inner, d_conv):
    kmax = SSD_CONV_K - 1

    @pl.when(pl.program_id(1) == 0)
    def _():
        halo_ref[...] = jnp.zeros_like(halo_ref)

    h = (_rms(x_ref[...]) * g_ref[...] * (1.0 + mod_ref[1:2, :]) + mod_ref[0:1, :]).astype(BF16)
    dt_ref[...] = _dot(h, wdt_ref[...])

    sub = 256
    for c0 in range(0, w_ref.shape[1], sub):
        cols = slice(c0, c0 + sub)
        pre = _dot(h, w_ref[:, cols])
        if c0 < d_inner:
            o_ref[:, cols] = _silu_of_double(pre).astype(BF16)
        elif c0 < d_inner + d_conv:
            cc = slice(c0 - d_inner, c0 - d_inner + sub)
            halo = halo_ref[:, cc]
            halo_ref[:, cc] = pre[tm - kmax * SUBLANE:tm]
            half_acc = _causal_conv(pre, halo, cw_ref, cc.start, sub, SSD_CONV_K) + 0.5 * cb_ref[:, cc]
            o_ref[:, cols] = _silu_of_double(half_acc).astype(BF16)
        else:
            o_ref[:, cols] = pre.astype(BF16)


def _inproj_call(x, mod3, g, w_main, w_dt, conv_w, conv_b, tm, d_inner):
    bsz, seq, d = x.shape
    n = w_main.shape[1]
    d_conv = conv_w.shape[1]
    kern = functools.partial(_inproj_kernel, tm=tm, d_inner=d_inner, d_conv=d_conv)
    const = lambda b, i: (0, 0)
    return pl.pallas_call(
        kern,
        grid=(bsz, seq // tm),
        in_specs=[
            pl.BlockSpec((None, tm, d), lambda b, i: (b, i, 0)),
            pl.BlockSpec((None, 6, d), lambda b, i: (b, 0, 0)),
            pl.BlockSpec((1, d), const),
            pl.BlockSpec((d, n), const, pipeline_mode=pl.Buffered(1)),
            pl.BlockSpec((d, LANE), const),
            pl.BlockSpec((SSD_CONV_K, d_conv), const),
            pl.BlockSpec((1, d_conv), const),
        ],
        out_specs=[
            pl.BlockSpec((None, tm, n), lambda b, i: (b, i, 0)),
            pl.BlockSpec((None, tm, LANE), lambda b, i: (b, i, 0)),
        ],
        out_shape=[
            jax.ShapeDtypeStruct((bsz, seq, n), BF16),
            jax.ShapeDtypeStruct((bsz, seq, LANE), F32),
        ],
        scratch_shapes=[
            pltpu.VMEM(((SSD_CONV_K - 1) * SUBLANE, d_conv), F32),
        ],
        compiler_params=pltpu.CompilerParams(
            dimension_semantics=("parallel", "arbitrary"), vmem_limit_bytes=VMEM_LIMIT),
        name="inproj",
    )(x, mod3, g, w_main, w_dt, conv_w, conv_b)


def _ssd_kernel(zs_ref, xc_ref, bc_ref, dtr_ref, dtb_ref, a_ref, dexp_ref, ng_ref, tril_ref, exp_ref, o_ref,
                state_ref, a2t_ref, y_ref, *, tc, d_inner):
    n_heads = d_inner // HEAD_DIM
    hpg = n_heads // N_GROUPS
    gw = hpg * HEAD_DIM

    @pl.when(pl.program_id(1) == 0)
    def _():
        state_ref[...] = jnp.zeros_like(state_ref)

    tok_r = _token_of_row(lax.broadcasted_iota(jnp.int32, (BLOCK, BLOCK), 0))
    tok_c = _token_of_row(lax.broadcasted_iota(jnp.int32, (BLOCK, BLOCK), 1))
    causal = tok_r >= tok_c
    low_half = lax.broadcasted_iota(jnp.int32, (BLOCK, LANE), 1) < HEAD_DIM
    c_off = N_GROUPS * D_STATE

    def chunk_body(ci, carry):
        r0 = pl.multiple_of(ci * BLOCK, BLOCK)
        rows = pl.ds(r0, BLOCK)

        dtv = dtr_ref[rows, :] + dtb_ref[...]
        dt = jnp.maximum(dtv, 0.0) + jnp.log(1.0 + jnp.exp(-jnp.abs(dtv)))
        adt_hi, adt_lo = _split_bf16(dt * a_ref[...])
        a_cs = _dot(tril_ref[...], jnp.concatenate([adt_hi, adt_lo], axis=0))
        a_last = a_cs[BLOCK - 1:BLOCK, :]
        a2 = a_cs * LOG2E
        a2t_ref[...] = a2.T

        factors = jnp.concatenate(
            [jnp.exp(a_last - a_cs), dt, jnp.broadcast_to(jnp.exp(a_last), (2 * SUBLANE, LANE))], axis=0)
        f_hi, f_lo = _split_bf16(factors)
        fac = _dot(jnp.concatenate([f_hi, f_lo], axis=1), exp_ref[...])
        dec_exp = fac[0:BLOCK, :]
        dt_exp = fac[BLOCK:2 * BLOCK, :]
        cd_exp = fac[2 * BLOCK:2 * BLOCK + 1, :]

        for g in range(N_GROUPS):
            gsl = slice(g * gw, (g + 1) * gw)
            bg = bc_ref[rows, g * D_STATE:(g + 1) * D_STATE]
            cgb = bc_ref[rows, c_off + g * D_STATE:c_off + (g + 1) * D_STATE]
            cg = cgb.astype(F32)
            scores = lax.dot_general(cgb, bg, (((1,), (1,)), ((), ())), preferred_element_type=F32)
            scores = jnp.where(causal, scores, 0.0)
            xg = xc_ref[rows, gsl].astype(F32)
            xdt = xg * dt_exp[:, gsl]
            hg = state_ref[:, gsl]
            for j in range(hpg // 2):
                lanes = slice(j * LANE, (j + 1) * LANE)
                rhs = jnp.concatenate([xdt[:, lanes].astype(BF16), hg[:, lanes].astype(BF16)], axis=0)
                outs = []
                for q in range(2):
                    r = g * hpg + 2 * j + q
                    a2_col = jnp.broadcast_to(a2[:, r:r + 1], (BLOCK, BLOCK))
                    decay = jnp.exp2(jnp.minimum(a2_col - a2t_ref[r:r + 1, :], 0.0))
                    m = scores * decay
                    ce = cg * jnp.exp2(a2_col)
                    lhs = jnp.concatenate([m.astype(BF16), ce.astype(BF16)], axis=1)
                    outs.append(_dot(lhs, rhs))
                y_pair = jnp.where(low_half, outs[0], outs[1])
                gl = slice(g * gw + j * LANE, g * gw + (j + 1) * LANE)
                y_ref[:, gl] = y_pair + dexp_ref[:, gl] * xg[:, lanes]
            xw = (xdt * dec_exp[:, gsl]).astype(BF16)
            upd = _dot(bg.astype(F32).T.astype(BF16), xw)
            state_ref[:, gsl] = cd_exp[:, gsl] * hg + upd

        y = y_ref[...] * zs_ref[rows, :].astype(F32)
        for g in range(N_GROUPS):
            gsl = slice(g * gw, (g + 1) * gw)
            o_ref[rows, gsl] = (_rms(y[:, gsl]) * ng_ref[:, gsl]).astype(BF16)
        return carry

    lax.fori_loop(0, tc // BLOCK, chunk_body, 0)


def _ssd_call(proj, dt_raw, dt_bias_row, a_row, d_exp, norm_g, tril2, expand2, tc, d_inner):
    bsz, seq, _ = proj.shape
    d_bc = 2 * N_GROUPS * D_STATE
    kern = functools.partial(_ssd_kernel, tc=tc, d_inner=d_inner)
    const = lambda b, s: (0, 0)
    return pl.pallas_call(
        kern,
        grid=(bsz, seq // tc),
        in_specs=[
            pl.BlockSpec((None, tc, d_inner), lambda b, s: (b, s, 0)),
            pl.BlockSpec((None, tc, d_inner), lambda b, s: (b, s, 1)),
            pl.BlockSpec((None, tc, d_bc), lambda b, s: (b, s, 2 * d_inner // d_bc)),
            pl.BlockSpec((None, tc, LANE), lambda b, s: (b, s, 0)),
            pl.BlockSpec((1, LANE), const),
            pl.BlockSpec((1, LANE), const),
            pl.BlockSpec((1, d_inner), const),
            pl.BlockSpec((1, d_inner), const),
            pl.BlockSpec((BLOCK, 2 * BLOCK), const),
            pl.BlockSpec((2 * LANE, d_inner), const),
        ],
        out_specs=pl.BlockSpec((None, tc, d_inner), lambda b, s: (b, s, 0)),
        out_shape=jax.ShapeDtypeStruct((bsz, seq, d_inner), BF16),
        scratch_shapes=[
            pltpu.VMEM((D_STATE, d_inner), F32),
            pltpu.VMEM((LANE, BLOCK), F32),
            pltpu.VMEM((BLOCK, d_inner), F32),
        ],
        compiler_params=pltpu.CompilerParams(
            dimension_semantics=("parallel", "arbitrary"), vmem_limit_bytes=VMEM_LIMIT),
        name="ssd",
    )(proj, proj, proj, dt_raw, dt_bias_row, a_row, d_exp, norm_g, tril2, expand2)


def _mix_kernel(x_ref, mod_ref, y_ref, scb_ref, scc_ref, sch_ref, gs_ref, gc_ref,
                wso_ref, scw_ref, wsc_ref, wo_ref, pg_ref, o_ref, halo_ref, *, tm):
    d = x_ref.shape[1]

    @pl.when(pl.program_id(1) == 0)
    def _():
        halo_ref[...] = jnp.zeros_like(halo_ref)

    v = scc_ref[...].astype(F32) * sch_ref[...].astype(F32)
    halo = halo_ref[...]
    halo_ref[...] = v[tm - (SC_CONV_K - 1) * SUBLANE:tm]
    conv = _causal_conv(v, halo, scw_ref, 0, d, SC_CONV_K)
    u = (scb_ref[...].astype(F32) * conv).astype(BF16)
    y_sc = _dot(u, wsc_ref[...])
    y_ssd = _dot(y_ref[...], wso_ref[...])
    merged = (_sigmoid_of_double(gs_ref[...].astype(F32)) * y_ssd
              + _sigmoid_of_double(gc_ref[...].astype(F32)) * y_sc)
    mix = _dot(merged.astype(BF16), wo_ref[...])
    o_ref[...] = x_ref[...] + mod_ref[2:3, :] * (_rms(mix) * pg_ref[...])


def _mix_call(x, mod3, y, proj, w_ssd_out, sc_conv_w, w_sc_out, w_o, post_g, tm):
    bsz, seq, d = x.shape
    d_inner = y.shape[2]
    kern = functools.partial(_mix_kernel, tm=tm)
    const = lambda b, s: (0, 0)
    base = (2 * d_inner + 2 * N_GROUPS * D_STATE) // d

    def pblock(k):
        return pl.BlockSpec((None, tm, d), lambda b, s, k=k: (b, s, base + k))

    return pl.pallas_call(
        kern,
        grid=(bsz, seq // tm),
        in_specs=[
            pl.BlockSpec((None, tm, d), lambda b, s: (b, s, 0)),
            pl.BlockSpec((None, 6, d), lambda b, s: (b, 0, 0)),
            pl.BlockSpec((None, tm, d_inner), lambda b, s: (b, s, 0)),
            pblock(0), pblock(1), pblock(2), pblock(3), pblock(4),
            pl.BlockSpec((d_inner, d), const),
            pl.BlockSpec((SC_CONV_K, d), const),
            pl.BlockSpec((d, d), const),
            pl.BlockSpec((d, d), const),
            pl.BlockSpec((1, d), const),
        ],
        out_specs=pl.BlockSpec((None, tm, d), lambda b, s: (b, s, 0)),
        out_shape=jax.ShapeDtypeStruct((bsz, seq, d), F32),
        scratch_shapes=[pltpu.VMEM(((SC_CONV_K - 1) * SUBLANE, d), F32)],
        compiler_params=pltpu.CompilerParams(
            dimension_semantics=("parallel", "arbitrary"), vmem_limit_bytes=VMEM_LIMIT),
        name="mix",
    )(x, mod3, y, proj, proj, proj, proj, proj, w_ssd_out, sc_conv_w, w_sc_out, w_o, post_g)


def _ffn_kernel(x_ref, mod_ref, pre_ref, wup_ref, cw_ref, cb_ref, wdn_ref, pg_ref, o_ref,
                halo_ref, act_ref, *, tm, d_ff, cw):
    @pl.when(pl.program_id(1) == 0)
    def _():
        halo_ref[...] = jnp.zeros_like(halo_ref)

    x = x_ref[...]
    h = (_rms(x) * pre_ref[...] * (1.0 + mod_ref[4:5, :]) + mod_ref[3:4, :]).astype(BF16)

    def conv_cols(c0, bias_scale):
        pre = _dot(h, wup_ref[:, c0:c0 + cw])
        halo = halo_ref[:, c0:c0 + cw]
        halo_ref[:, c0:c0 + cw] = pre[tm - (FFN_CONV_K - 1) * SUBLANE:tm]
        bias = cb_ref[:, c0:c0 + cw]
        if bias_scale != 1.0:
            bias = bias_scale * bias
        return _causal_conv(pre, halo, cw_ref, c0, cw, FFN_CONV_K) + bias

    for c0 in range(0, d_ff, cw):
        half_gate = conv_cols(c0, 0.5)
        val = conv_cols(d_ff + c0, 1.0)
        act_ref[:, c0:c0 + cw] = (_silu_of_double(half_gate) * val).astype(BF16)
    f = _dot(act_ref[...], wdn_ref[...])
    o_ref[...] = x + mod_ref[5:6, :] * (_rms(f) * pg_ref[...])


def _ffn_call(x, mod3, pre_g, w_up, conv_w, conv_b, w_down, post_g, tm, cw):
    bsz, seq, d = x.shape
    d_ff = w_down.shape[0]
    kern = functools.partial(_ffn_kernel, tm=tm, d_ff=d_ff, cw=cw)
    const = lambda b, s: (0, 0)
    return pl.pallas_call(
        kern,
        grid=(bsz, seq // tm),
        in_specs=[
            pl.BlockSpec((None, tm, d), lambda b, s: (b, s, 0)),
            pl.BlockSpec((None, 6, d), lambda b, s: (b, 0, 0)),
            pl.BlockSpec((1, d), const),
            pl.BlockSpec((d, 2 * d_ff), const, pipeline_mode=pl.Buffered(1)),
            pl.BlockSpec((FFN_CONV_K, 2 * d_ff), const),
            pl.BlockSpec((1, 2 * d_ff), const),
            pl.BlockSpec((d_ff, d), const, pipeline_mode=pl.Buffered(1)),
            pl.BlockSpec((1, d), const),
        ],
        out_specs=pl.BlockSpec((None, tm, d), lambda b, s: (b, s, 0)),
        out_shape=jax.ShapeDtypeStruct((bsz, seq, d), F32),
        scratch_shapes=[
            pltpu.VMEM(((FFN_CONV_K - 1) * SUBLANE, 2 * d_ff), F32),
            pltpu.VMEM((tm, d_ff), BF16),
        ],
        compiler_params=pltpu.CompilerParams(
            dimension_semantics=("parallel", "arbitrary"), vmem_limit_bytes=VMEM_LIMIT),
        name="ffn",
    )(x, mod3, pre_g, w_up, conv_w, conv_b, w_down, post_g)


def _pick_tile(n, pref):
    t = min(n, pref)
    while n % t:
        t //= 2
    return t


def _to_block_order(x):
    b, s, d = x.shape
    return x.reshape(b, s // BLOCK, SUBLANE, SEG, d).transpose(0, 1, 3, 2, 4).reshape(b, s, d)


def _from_block_order(x):
    b, s, d = x.shape
    return x.reshape(b, s // BLOCK, SEG, SUBLANE, d).transpose(0, 1, 3, 2, 4).reshape(b, s, d)


def kernel(x, c, ada_w, ada_b, mix_pre_g, mix_post_g, w_in, ssd_conv_w, ssd_conv_b, ssd_dt_bias, ssd_a_log,
           ssd_d, ssd_norm_g, w_ssd_out, sc_conv_w, w_sc_out, w_o, ffn_pre_g, ffn_post_g, w_up, ffn_conv_w,
           ffn_conv_b, w_down):
    bsz, seq, d = x.shape
    depth = ada_w.shape[0]
    d_inner = ssd_norm_g.shape[1]
    n_heads = ssd_dt_bias.shape[1]
    d_conv = ssd_conv_w.shape[2]
    d_ff = w_down.shape[1]
    assert d_inner == n_heads * HEAD_DIM and d_conv == d_inner + 2 * N_GROUPS * D_STATE
    assert n_heads <= LANE and seq % BLOCK == 0

    mod = _ada_call(c, ada_w, ada_b).reshape(depth, bsz, 6, d)

    tok = _token_of_row(jnp.arange(BLOCK))
    tril = (tok[:, None] >= tok[None, :]).astype(BF16)
    tril2 = jnp.concatenate([tril, tril], axis=1)
    head_of_lane = jnp.arange(d_inner) // HEAD_DIM
    expand = (jnp.arange(LANE)[:, None] == head_of_lane[None, :]).astype(BF16)
    expand2 = jnp.concatenate([expand, expand], axis=0)

    tc = _pick_tile(seq, 512)
    tm = _pick_tile(seq, 512)

    x = _to_block_order(x)
    for i in range(depth):
        wi = w_in[i]
        o_dt = d_inner + d_conv
        o_sc = o_dt + n_heads
        o_gate = wi.shape[1] - 2 * d
        w_main = jnp.concatenate([0.5 * wi[:, :o_dt], wi[:, o_sc:o_gate], 0.5 * wi[:, o_gate:]],
                                 axis=1).astype(BF16)
        w_up_scaled = jnp.concatenate([0.5 * w_up[i][:, :d_ff], w_up[i][:, d_ff:]], axis=1).astype(BF16)
        w_dt = jnp.pad(wi[:, o_dt:o_sc], ((0, 0), (0, LANE - n_heads))).astype(BF16)
        pad_h = (0, LANE - n_heads)
        dt_bias_row = jnp.pad(ssd_dt_bias[i], pad_h).reshape(1, LANE)
        a_row = jnp.pad(-jnp.exp(ssd_a_log[i]), pad_h).reshape(1, LANE)
        d_exp = jnp.repeat(ssd_d[i], HEAD_DIM).reshape(1, d_inner)

        proj, dt_raw = _inproj_call(x, mod[i], mix_pre_g[i].reshape(1, d), w_main, w_dt, ssd_conv_w[i],
                                    ssd_conv_b[i].reshape(1, d_conv), tm, d_inner)
        y = _ssd_call(proj, dt_raw, dt_bias_row, a_row, d_exp, ssd_norm_g[i].reshape(1, d_inner), tril2, expand2,
                      tc, d_inner)
        x = _mix_call(x, mod[i], y, proj, w_ssd_out[i].astype(BF16), sc_conv_w[i], w_sc_out[i].astype(BF16),
                      w_o[i].astype(BF16), mix_post_g[i].reshape(1, d), tm)
        x = _ffn_call(x, mod[i], ffn_pre_g[i].reshape(1, d), w_up_scaled, ffn_conv_w[i],
                      ffn_conv_b[i].reshape(1, 2 * d_ff), w_down[i].astype(BF16), ffn_post_g[i].reshape(1, d),
                      tm, 256)
    return _from_block_order(x)
```

```python
import functools
import math

import jax
import jax.numpy as jnp
from jax import lax
from jax.experimental import pallas as pl
from jax.experimental.pallas import tpu as pltpu

F32 = jnp.float32
BF16 = jnp.bfloat16

EPS = 1e-6
LANE = 128
SUBLANE = 8
VMEM_LIMIT = 56 * 1024 * 1024

HEAD_DIM = 64
N_GROUPS = 4
D_STATE = 128
BLOCK = 128
SEG = BLOCK // SUBLANE
SSD_CONV_K = 4
SC_CONV_K = 3
FFN_CONV_K = 3
LOG2E = math.log2(math.e)


def _dot(a, b):
    return jnp.dot(a, b, preferred_element_type=F32)


def _sigmoid(v):
    return 1.0 / (1.0 + jnp.exp(-v))


def _silu(v):
    return v * _sigmoid(v)


def _silu_of_double(hv):
    return hv + hv * jnp.tanh(hv)


def _sigmoid_of_double(hv):
    return 0.5 + 0.5 * jnp.tanh(hv)


def _split_bf16(v):
    hi = v.astype(BF16)
    lo = (v - hi.astype(F32)).astype(BF16)
    return hi, lo


def _rms(v):
    return v * lax.rsqrt(jnp.mean(v * v, axis=-1, keepdims=True) + EPS)


def _token_of_row(r):
    return (r & (SUBLANE - 1)) * SEG + (r >> 3)


def _load_rows(x_ref, natural_order):
    if not natural_order:
        return x_ref[...]
    parts = []
    for b0 in range(0, x_ref.shape[0], BLOCK):
        for j in range(SEG):
            parts.append(x_ref[pl.ds(b0 + j, SUBLANE, stride=SEG), :])
    return jnp.concatenate(parts, axis=0)


def _store_rows(o_ref, val, natural_order):
    if not natural_order:
        o_ref[...] = val
        return
    for b0 in range(0, o_ref.shape[0], BLOCK):
        for j in range(SEG):
            r = b0 + j * SUBLANE
            o_ref[pl.ds(b0 + j, SUBLANE, stride=SEG), :] = val[r:r + SUBLANE]


def _block_shifts(cur, prev_tail, kmax):
    n = cur.shape[0]
    tail = cur[n - kmax * SUBLANE:n]
    first_sublane = lax.broadcasted_iota(jnp.int32, (SUBLANE, cur.shape[1]), 0) == 0
    wrapped = []
    for m in range(kmax):
        rows = slice(m * SUBLANE, (m + 1) * SUBLANE)
        wrapped.append(jnp.where(first_sublane, pltpu.roll(prev_tail[rows], 1, 0), pltpu.roll(tail[rows], 1, 0)))
    out = [cur]
    for k in range(1, kmax + 1):
        out.append(jnp.concatenate(wrapped[kmax - k:] + [cur[0:n - k * SUBLANE]], axis=0))
    return out


def _causal_conv(val, halo, w_ref, c0, width, kw):
    kmax = kw - 1
    outs = []
    for b0 in range(0, val.shape[0], BLOCK):
        cur = val[b0:b0 + BLOCK]
        prev_tail = halo if b0 == 0 else val[b0 - kmax * SUBLANE:b0]
        sh = _block_shifts(cur, prev_tail, kmax)
        acc = sh[0] * w_ref[kmax:kmax + 1, c0:c0 + width]
        for k in range(1, kw):
            acc = acc + sh[k] * w_ref[kmax - k:kmax - k + 1, c0:c0 + width]
        outs.append(acc)
    return outs[0] if len(outs) == 1 else jnp.concatenate(outs, axis=0)


def _ada_kernel(c_ref, w_ref, b_ref, o_ref):
    ca = _silu(c_ref[...])
    c_hi, c_lo = _split_bf16(ca)
    w_hi, w_lo = _split_bf16(w_ref[...])
    acc = _dot(c_hi, w_hi) + _dot(c_hi, w_lo) + _dot(c_lo, w_hi)
    o_ref[...] = acc + b_ref[...]


def _ada_call(c, ada_w, ada_b):
    depth, d, n = ada_w.shape
    bsz = c.shape[0]
    tn = 1024
    return pl.pallas_call(
        _ada_kernel,
        grid=(depth, n // tn),
        in_specs=[
            pl.BlockSpec((bsz, d), lambda l, j: (0, 0)),
            pl.BlockSpec((None, d, tn), lambda l, j: (l, 0, j)),
            pl.BlockSpec((None, 1, tn), lambda l, j: (l, 0, j)),
        ],
        out_specs=pl.BlockSpec((None, bsz, tn), lambda l, j: (l, 0, j)),
        out_shape=jax.ShapeDtypeStruct((depth, bsz, n), F32),
        compiler_params=pltpu.CompilerParams(
            dimension_semantics=("arbitrary", "arbitrary"), vmem_limit_bytes=VMEM_LIMIT),
        name="ada_mod",
    )(c, ada_w, ada_b.reshape(depth, 1, n))


def _inproj_kernel(x_ref, mod_ref, g_ref, w_ref, wdt_ref, cw_ref, cb_ref, o_ref, dt_ref, halo_ref,
                   *, tm, d_inner, d_conv, x_natural):
    kmax = SSD_CONV_K - 1

    @pl.when(pl.program_id(1) == 0)
    def _():
        halo_ref[...] = jnp.zeros_like(halo_ref)

    x = _load_rows(x_ref, x_natural)
    h = (_rms(x) * g_ref[...] * (1.0 + mod_ref[1:2, :]) + mod_ref[0:1, :]).astype(BF16)
    dt_ref[...] = _dot(h, wdt_ref[...])

    sub = 256
    for c0 in range(0, w_ref.shape[1], sub):
        cols = slice(c0, c0 + sub)
        pre = _dot(h, w_ref[:, cols])
        if c0 < d_inner:
            o_ref[:, cols] = _silu_of_double(pre).astype(BF16)
        elif c0 < d_inner + d_conv:
            cc = slice(c0 - d_inner, c0 - d_inner + sub)
            halo = halo_ref[:, cc]
            halo_ref[:, cc] = pre[tm - kmax * SUBLANE:tm]
            half_acc = _causal_conv(pre, halo, cw_ref, cc.start, sub, SSD_CONV_K) + 0.5 * cb_ref[:, cc]
            o_ref[:, cols] = _silu_of_double(half_acc).astype(BF16)
        else:
            o_ref[:, cols] = pre.astype(BF16)


def _inproj_call(x, mod3, g, w_main, w_dt, conv_w, conv_b, tm, d_inner, x_natural):
    bsz, seq, d = x.shape
    n = w_main.shape[1]
    d_conv = conv_w.shape[1]
    kern = functools.partial(_inproj_kernel, tm=tm, d_inner=d_inner, d_conv=d_conv, x_natural=x_natural)
    const = lambda b, i: (0, 0)
    return pl.pallas_call(
        kern,
        grid=(bsz, seq // tm),
        in_specs=[
            pl.BlockSpec((None, tm, d), lambda b, i: (b, i, 0)),
            pl.BlockSpec((None, 6, d), lambda b, i: (b, 0, 0)),
            pl.BlockSpec((1, d), const),
            pl.BlockSpec((d, n), const, pipeline_mode=pl.Buffered(1)),
            pl.BlockSpec((d, LANE), const),
            pl.BlockSpec((SSD_CONV_K, d_conv), const),
            pl.BlockSpec((1, d_conv), const),
        ],
        out_specs=[
            pl.BlockSpec((None, tm, n), lambda b, i: (b, i, 0)),
            pl.BlockSpec((None, tm, LANE), lambda b, i: (b, i, 0)),
        ],
        out_shape=[
            jax.ShapeDtypeStruct((bsz, seq, n), BF16),
            jax.ShapeDtypeStruct((bsz, seq, LANE), F32),
        ],
        scratch_shapes=[
            pltpu.VMEM(((SSD_CONV_K - 1) * SUBLANE, d_conv), F32),
        ],
        compiler_params=pltpu.CompilerParams(
            dimension_semantics=("parallel", "arbitrary"), vmem_limit_bytes=VMEM_LIMIT),
        name="inproj",
    )(x, mod3, g, w_main, w_dt, conv_w, conv_b)


def _ssd_kernel(zs_ref, xc_ref, bc_ref, dtr_ref, dtb_ref, a_ref, dexp_ref, ng_ref, tril_ref, exp_ref, o_ref,
                state_ref, a2t_ref, y_ref, *, tc, d_inner):
    n_heads = d_inner // HEAD_DIM
    hpg = n_heads // N_GROUPS
    gw = hpg * HEAD_DIM

    @pl.when(pl.program_id(1) == 0)
    def _():
        state_ref[...] = jnp.zeros_like(state_ref)

    tok_r = _token_of_row(lax.broadcasted_iota(jnp.int32, (BLOCK, BLOCK), 0))
    tok_c = _token_of_row(lax.broadcasted_iota(jnp.int32, (BLOCK, BLOCK), 1))
    causal = tok_r >= tok_c
    low_half = lax.broadcasted_iota(jnp.int32, (BLOCK, LANE), 1) < HEAD_DIM
    c_off = N_GROUPS * D_STATE

    def chunk_body(ci, carry):
        r0 = pl.multiple_of(ci * BLOCK, BLOCK)
        rows = pl.ds(r0, BLOCK)

        dtv = dtr_ref[rows, :] + dtb_ref[...]
        dt = jnp.maximum(dtv, 0.0) + jnp.log(1.0 + jnp.exp(-jnp.abs(dtv)))
        adt_hi, adt_lo = _split_bf16(dt * a_ref[...])
        a_cs = _dot(tril_ref[...], jnp.concatenate([adt_hi, adt_lo], axis=0))
        a_last = a_cs[BLOCK - 1:BLOCK, :]
        a2 = a_cs * LOG2E
        a2t_ref[...] = a2.T

        factors = jnp.concatenate(
            [jnp.exp(a_last - a_cs), dt, jnp.broadcast_to(jnp.exp(a_last), (2 * SUBLANE, LANE))], axis=0)
        f_hi, f_lo = _split_bf16(factors)
        fac = _dot(jnp.concatenate([f_hi, f_lo], axis=1), exp_ref[...])
        dec_exp = fac[0:BLOCK, :]
        dt_exp = fac[BLOCK:2 * BLOCK, :]
        cd_exp = fac[2 * BLOCK:2 * BLOCK + 1, :]

        for g in range(N_GROUPS):
            gsl = slice(g * gw, (g + 1) * gw)
            bg = bc_ref[rows, g * D_STATE:(g + 1) * D_STATE]
            cgb = bc_ref[rows, c_off + g * D_STATE:c_off + (g + 1) * D_STATE]
            cg = cgb.astype(F32)
            scores = lax.dot_general(cgb, bg, (((1,), (1,)), ((), ())), preferred_element_type=F32)
            scores = jnp.where(causal, scores, 0.0)
            xg = xc_ref[rows, gsl].astype(F32)
            xdt = xg * dt_exp[:, gsl]
            hg = state_ref[:, gsl]
            for j in range(hpg // 2):
                lanes = slice(j * LANE, (j + 1) * LANE)
                rhs = jnp.concatenate([xdt[:, lanes].astype(BF16), hg[:, lanes].astype(BF16)], axis=0)
                outs = []
                for q in range(2):
                    r = g * hpg + 2 * j + q
                    a2_col = jnp.broadcast_to(a2[:, r:r + 1], (BLOCK, BLOCK))
                    decay = jnp.exp2(jnp.minimum(a2_col - a2t_ref[r:r + 1, :], 0.0))
                    m = scores * decay
                    ce = cg * jnp.exp2(a2_col)
                    lhs = jnp.concatenate([m.astype(BF16), ce.astype(BF16)], axis=1)
                    outs.append(_dot(lhs, rhs))
                y_pair = jnp.where(low_half, outs[0], outs[1])
                gl = slice(g * gw + j * LANE, g * gw + (j + 1) * LANE)
                y_ref[:, gl] = y_pair + dexp_ref[:, gl] * xg[:, lanes]
            xw = (xdt * dec_exp[:, gsl]).astype(BF16)
            upd = _dot(bg.astype(F32).T.astype(BF16), xw)
            state_ref[:, gsl] = cd_exp[:, gsl] * hg + upd

        y = y_ref[...] * zs_ref[rows, :].astype(F32)
        for g in range(N_GROUPS):
            gsl = slice(g * gw, (g + 1) * gw)
            o_ref[rows, gsl] = (_rms(y[:, gsl]) * ng_ref[:, gsl]).astype(BF16)
        return carry

    lax.fori_loop(0, tc // BLOCK, chunk_body, 0)


def _ssd_call(proj, dt_raw, dt_bias_row, a_row, d_exp, norm_g, tril2, expand2, tc, d_inner):
    bsz, seq, _ = proj.shape
    d_bc = 2 * N_GROUPS * D_STATE
    kern = functools.partial(_ssd_kernel, tc=tc, d_inner=d_inner)
    const = lambda b, s: (0, 0)
    return pl.pallas_call(
        kern,
        grid=(bsz, seq // tc),
        in_specs=[
            pl.BlockSpec((None, tc, d_inner), lambda b, s: (b, s, 0)),
            pl.BlockSpec((None, tc, d_inner), lambda b, s: (b, s, 1)),
            pl.BlockSpec((None, tc, d_bc), lambda b, s: (b, s, 2 * d_inner // d_bc)),
            pl.BlockSpec((None, tc, LANE), lambda b, s: (b, s, 0)),
            pl.BlockSpec((1, LANE), const),
            pl.BlockSpec((1, LANE), const),
            pl.BlockSpec((1, d_inner), const),
            pl.BlockSpec((1, d_inner), const),
            pl.BlockSpec((BLOCK, 2 * BLOCK), const),
            pl.BlockSpec((2 * LANE, d_inner), const),
        ],
        out_specs=pl.BlockSpec((None, tc, d_inner), lambda b, s: (b, s, 0)),
        out_shape=jax.ShapeDtypeStruct((bsz, seq, d_inner), BF16),
        scratch_shapes=[
            pltpu.VMEM((D_STATE, d_inner), F32),
            pltpu.VMEM((LANE, BLOCK), F32),
            pltpu.VMEM((BLOCK, d_inner), F32),
        ],
        compiler_params=pltpu.CompilerParams(
            dimension_semantics=("parallel", "arbitrary"), vmem_limit_bytes=VMEM_LIMIT),
        name="ssd",
    )(proj, proj, proj, dt_raw, dt_bias_row, a_row, d_exp, norm_g, tril2, expand2)


def _mix_kernel(x_ref, mod_ref, y_ref, scb_ref, scc_ref, sch_ref, gs_ref, gc_ref,
                wso_ref, scw_ref, wsc_ref, wo_ref, pg_ref, o_ref, halo_ref, *, tm, x_natural):
    d = x_ref.shape[1]

    @pl.when(pl.program_id(1) == 0)
    def _():
        halo_ref[...] = jnp.zeros_like(halo_ref)

    v = scc_ref[...].astype(F32) * sch_ref[...].astype(F32)
    halo = halo_ref[...]
    halo_ref[...] = v[tm - (SC_CONV_K - 1) * SUBLANE:tm]
    conv = _causal_conv(v, halo, scw_ref, 0, d, SC_CONV_K)
    u = (scb_ref[...].astype(F32) * conv).astype(BF16)
    y_sc = _dot(u, wsc_ref[...])
    y_ssd = _dot(y_ref[...], wso_ref[...])
    merged = (_sigmoid_of_double(gs_ref[...].astype(F32)) * y_ssd
              + _sigmoid_of_double(gc_ref[...].astype(F32)) * y_sc)
    mix = _dot(merged.astype(BF16), wo_ref[...])
    o_ref[...] = _load_rows(x_ref, x_natural) + mod_ref[2:3, :] * (_rms(mix) * pg_ref[...])


def _mix_call(x, mod3, y, proj, w_ssd_out, sc_conv_w, w_sc_out, w_o, post_g, tm, x_natural):
    bsz, seq, d = x.shape
    d_inner = y.shape[2]
    kern = functools.partial(_mix_kernel, tm=tm, x_natural=x_natural)
    const = lambda b, s: (0, 0)
    base = (2 * d_inner + 2 * N_GROUPS * D_STATE) // d

    def pblock(k):
        return pl.BlockSpec((None, tm, d), lambda b, s, k=k: (b, s, base + k))

    return pl.pallas_call(
        kern,
        grid=(bsz, seq // tm),
        in_specs=[
            pl.BlockSpec((None, tm, d), lambda b, s: (b, s, 0)),
            pl.BlockSpec((None, 6, d), lambda b, s: (b, 0, 0)),
            pl.BlockSpec((None, tm, d_inner), lambda b, s: (b, s, 0)),
            pblock(0), pblock(1), pblock(2), pblock(3), pblock(4),
            pl.BlockSpec((d_inner, d), const),
            pl.BlockSpec((SC_CONV_K, d), const),
            pl.BlockSpec((d, d), const),
            pl.BlockSpec((d, d), const),
            pl.BlockSpec((1, d), const),
        ],
        out_specs=pl.BlockSpec((None, tm, d), lambda b, s: (b, s, 0)),
        out_shape=jax.ShapeDtypeStruct((bsz, seq, d), F32),
        scratch_shapes=[pltpu.VMEM(((SC_CONV_K - 1) * SUBLANE, d), F32)],
        compiler_params=pltpu.CompilerParams(
            dimension_semantics=("parallel", "arbitrary"), vmem_limit_bytes=VMEM_LIMIT),
        name="mix",
    )(x, mod3, y, proj, proj, proj, proj, proj, w_ssd_out, sc_conv_w, w_sc_out, w_o, post_g)


def _ffn_kernel(x_ref, mod_ref, pre_ref, wup_ref, cw_ref, cb_ref, wdn_ref, pg_ref, o_ref,
                halo_ref, act_ref, *, tm, d_ff, cw, out_natural):
    @pl.when(pl.program_id(1) == 0)
    def _():
        halo_ref[...] = jnp.zeros_like(halo_ref)

    x = x_ref[...]
    h = (_rms(x) * pre_ref[...] * (1.0 + mod_ref[4:5, :]) + mod_ref[3:4, :]).astype(BF16)

    def conv_cols(c0, bias_scale):
        pre = _dot(h, wup_ref[:, c0:c0 + cw])
        halo = halo_ref[:, c0:c0 + cw]
        halo_ref[:, c0:c0 + cw] = pre[tm - (FFN_CONV_K - 1) * SUBLANE:tm]
        bias = cb_ref[:, c0:c0 + cw]
        if bias_scale != 1.0:
            bias = bias_scale * bias
        return _causal_conv(pre, halo, cw_ref, c0, cw, FFN_CONV_K) + bias

    for c0 in range(0, d_ff, cw):
        half_gate = conv_cols(c0, 0.5)
        val = conv_cols(d_ff + c0, 1.0)
        act_ref[:, c0:c0 + cw] = (_silu_of_double(half_gate) * val).astype(BF16)
    f = _dot(act_ref[...], wdn_ref[...])
    _store_rows(o_ref, x + mod_ref[5:6, :] * (_rms(f) * pg_ref[...]), out_natural)


def _ffn_call(x, mod3, pre_g, w_up, conv_w, conv_b, w_down, post_g, tm, cw, out_natural):
    bsz, seq, d = x.shape
    d_ff = w_down.shape[0]
    kern = functools.partial(_ffn_kernel, tm=tm, d_ff=d_ff, cw=cw, out_natural=out_natural)
    const = lambda b, s: (0, 0)
    return pl.pallas_call(
        kern,
        grid=(bsz, seq // tm),
        in_specs=[
            pl.BlockSpec((None, tm, d), lambda b, s: (b, s, 0)),
            pl.BlockSpec((None, 6, d), lambda b, s: (b, 0, 0)),
            pl.BlockSpec((1, d), const),
            pl.BlockSpec((d, 2 * d_ff), const, pipeline_mode=pl.Buffered(1)),
            pl.BlockSpec((FFN_CONV_K, 2 * d_ff), const),
            pl.BlockSpec((1, 2 * d_ff), const),
            pl.BlockSpec((d_ff, d), const, pipeline_mode=pl.Buffered(1)),
            pl.BlockSpec((1, d), const),
        ],
        out_specs=pl.BlockSpec((None, tm, d), lambda b, s: (b, s, 0)),
        out_shape=jax.ShapeDtypeStruct((bsz, seq, d), F32),
        scratch_shapes=[
            pltpu.VMEM(((FFN_CONV_K - 1) * SUBLANE, 2 * d_ff), F32),
            pltpu.VMEM((tm, d_ff), BF16),
        ],
        compiler_params=pltpu.CompilerParams(
            dimension_semantics=("parallel", "arbitrary"), vmem_limit_bytes=VMEM_LIMIT),
        name="ffn",
    )(x, mod3, pre_g, w_up, conv_w, conv_b, w_down, post_g)


def _pick_tile(n, pref):
    t = min(n, pref)
    while n % t:
        t //= 2
    return t


def _to_block_order(x):
    b, s, d = x.shape
    return x.reshape(b, s // BLOCK, SUBLANE, SEG, d).transpose(0, 1, 3, 2, 4).reshape(b, s, d)


def _from_block_order(x):
    b, s, d = x.shape
    return x.reshape(b, s // BLOCK, SEG, SUBLANE, d).transpose(0, 1, 3, 2, 4).reshape(b, s, d)


def _win_prep_kernel(w_ref, main_ref, dt_ref, *, o_dt, n_heads, o_gate):
    o_sc = o_dt + n_heads
    n_sc = o_gate - o_sc
    n_all = w_ref.shape[1]
    main_ref[:, 0:o_dt] = (0.5 * w_ref[:, 0:o_dt]).astype(BF16)
    main_ref[:, o_dt:o_dt + n_sc] = w_ref[:, o_sc:o_gate].astype(BF16)
    main_ref[:, o_dt + n_sc:] = (0.5 * w_ref[:, o_gate:n_all]).astype(BF16)
    pad = jnp.zeros((w_ref.shape[0], LANE - n_heads), F32)
    dt_ref[...] = jnp.concatenate([w_ref[:, o_dt:o_sc], pad], axis=1).astype(BF16)


def _win_prep_call(w_in, o_dt, n_heads, d):
    depth, k, n_all = w_in.shape
    o_gate = n_all - 2 * d
    n_main = n_all - n_heads
    rb = 128
    kern = functools.partial(_win_prep_kernel, o_dt=o_dt, n_heads=n_heads, o_gate=o_gate)
    return pl.pallas_call(
        kern,
        grid=(depth, k // rb),
        in_specs=[pl.BlockSpec((None, rb, n_all), lambda l, r: (l, r, 0))],
        out_specs=[
            pl.BlockSpec((None, rb, n_main), lambda l, r: (l, r, 0)),
            pl.BlockSpec((None, rb, LANE), lambda l, r: (l, r, 0)),
        ],
        out_shape=[
            jax.ShapeDtypeStruct((depth, k, n_main), BF16),
            jax.ShapeDtypeStruct((depth, k, LANE), BF16),
        ],
        compiler_params=pltpu.CompilerParams(
            dimension_semantics=("arbitrary", "arbitrary"), vmem_limit_bytes=VMEM_LIMIT),
        name="win_prep",
    )(w_in)


def kernel(x, c, ada_w, ada_b, mix_pre_g, mix_post_g, w_in, ssd_conv_w, ssd_conv_b, ssd_dt_bias, ssd_a_log,
           ssd_d, ssd_norm_g, w_ssd_out, sc_conv_w, w_sc_out, w_o, ffn_pre_g, ffn_post_g, w_up, ffn_conv_w,
           ffn_conv_b, w_down):
    bsz, seq, d = x.shape
    depth = ada_w.shape[0]
    d_inner = ssd_norm_g.shape[1]
    n_heads = ssd_dt_bias.shape[1]
    d_conv = ssd_conv_w.shape[2]
    d_ff = w_down.shape[1]
    assert d_inner == n_heads * HEAD_DIM and d_conv == d_inner + 2 * N_GROUPS * D_STATE
    assert n_heads <= LANE and seq % BLOCK == 0

    mod = _ada_call(c, ada_w, ada_b).reshape(depth, bsz, 6, d)

    tok = _token_of_row(jnp.arange(BLOCK))
    tril = (tok[:, None] >= tok[None, :]).astype(BF16)
    tril2 = jnp.concatenate([tril, tril], axis=1)
    head_of_lane = jnp.arange(d_inner) // HEAD_DIM
    expand = (jnp.arange(LANE)[:, None] == head_of_lane[None, :]).astype(BF16)
    expand2 = jnp.concatenate([expand, expand], axis=0)

    tc = _pick_tile(seq, 512)
    tm = _pick_tile(seq, 512)

    w_main_all, w_dt_all = _win_prep_call(w_in, d_inner + d_conv, n_heads, d)
    up_scale = jnp.where(jnp.arange(2 * d_ff) < d_ff, 0.5, 1.0).astype(F32)
    w_up_all = (w_up * up_scale).astype(BF16)
    w_down_all = w_down.astype(BF16)
    w_ssd_out_all = w_ssd_out.astype(BF16)
    w_sc_out_all = w_sc_out.astype(BF16)
    w_o_all = w_o.astype(BF16)

    x = _to_block_order(x)
    for i in range(depth):
        first, last = False, False
        w_main, w_dt, w_up_scaled = w_main_all[i], w_dt_all[i], w_up_all[i]
        pad_h = (0, LANE - n_heads)
        dt_bias_row = jnp.pad(ssd_dt_bias[i], pad_h).reshape(1, LANE)
        a_row = jnp.pad(-jnp.exp(ssd_a_log[i]), pad_h).reshape(1, LANE)
        d_exp = jnp.repeat(ssd_d[i], HEAD_DIM).reshape(1, d_inner)

        proj, dt_raw = _inproj_call(x, mod[i], mix_pre_g[i].reshape(1, d), w_main, w_dt, ssd_conv_w[i],
                                    ssd_conv_b[i].reshape(1, d_conv), tm, d_inner, first)
        y = _ssd_call(proj, dt_raw, dt_bias_row, a_row, d_exp, ssd_norm_g[i].reshape(1, d_inner), tril2, expand2,
                      tc, d_inner)
        x = _mix_call(x, mod[i], y, proj, w_ssd_out_all[i], sc_conv_w[i], w_sc_out_all[i],
                      w_o_all[i], mix_post_g[i].reshape(1, d), tm, first)
        x = _ffn_call(x, mod[i], ffn_pre_g[i].reshape(1, d), w_up_scaled, ffn_conv_w[i],
                      ffn_conv_b[i].reshape(1, 2 * d_ff), w_down_all[i], ffn_post_g[i].reshape(1, d),
                      tm, 256, last)
    return _from_block_order(x)
```

```python
import functools
import math

import jax
import jax.numpy as jnp
from jax import lax
from jax.experimental import pallas as pl
from jax.experimental.pallas import tpu as pltpu

F32 = jnp.float32
BF16 = jnp.bfloat16

EPS = 1e-6
LANE = 128
SUBLANE = 8
VMEM_LIMIT = 56 * 1024 * 1024

HEAD_DIM = 64
N_GROUPS = 4
D_STATE = 128
BLOCK = 128
SEG = BLOCK // SUBLANE
SSD_CONV_K = 4
SC_CONV_K = 3
FFN_CONV_K = 3
LOG2E = math.log2(math.e)


def _dot(a, b):
    return jnp.dot(a, b, preferred_element_type=F32)


def _sigmoid(v):
    return 1.0 / (1.0 + jnp.exp(-v))


def _silu(v):
    return v * _sigmoid(v)


def _silu_of_double(hv):
    return hv + hv * jnp.tanh(hv)


def _sigmoid_of_double(hv):
    return 0.5 + 0.5 * jnp.tanh(hv)


def _split_bf16(v):
    hi = v.astype(BF16)
    lo = (v - hi.astype(F32)).astype(BF16)
    return hi, lo


def _rms(v):
    return v * lax.rsqrt(jnp.mean(v * v, axis=-1, keepdims=True) + EPS)


def _token_of_row(r):
    return (r & (SUBLANE - 1)) * SEG + (r >> 3)


def _load_rows(x_ref, natural_order):
    if not natural_order:
        return x_ref[...]
    parts = []
    for b0 in range(0, x_ref.shape[0], BLOCK):
        for j in range(SEG):
            parts.append(x_ref[pl.ds(b0 + j, SUBLANE, stride=SEG), :])
    return jnp.concatenate(parts, axis=0)


def _store_rows(o_ref, val, natural_order):
    if not natural_order:
        o_ref[...] = val
        return
    for b0 in range(0, o_ref.shape[0], BLOCK):
        for j in range(SEG):
            r = b0 + j * SUBLANE
            o_ref[pl.ds(b0 + j, SUBLANE, stride=SEG), :] = val[r:r + SUBLANE]


def _block_shifts(cur, prev_tail, kmax):
    n = cur.shape[0]
    tail = cur[n - kmax * SUBLANE:n]
    first_sublane = lax.broadcasted_iota(jnp.int32, (SUBLANE, cur.shape[1]), 0) == 0
    wrapped = []
    for m in range(kmax):
        rows = slice(m * SUBLANE, (m + 1) * SUBLANE)
        wrapped.append(jnp.where(first_sublane, pltpu.roll(prev_tail[rows], 1, 0), pltpu.roll(tail[rows], 1, 0)))
    out = [cur]
    for k in range(1, kmax + 1):
        out.append(jnp.concatenate(wrapped[kmax - k:] + [cur[0:n - k * SUBLANE]], axis=0))
    return out


def _causal_conv(val, halo, w_ref, c0, width, kw):
    kmax = kw - 1
    outs = []
    for b0 in range(0, val.shape[0], BLOCK):
        cur = val[b0:b0 + BLOCK]
        prev_tail = halo if b0 == 0 else val[b0 - kmax * SUBLANE:b0]
        sh = _block_shifts(cur, prev_tail, kmax)
        acc = sh[0] * w_ref[kmax:kmax + 1, c0:c0 + width]
        for k in range(1, kw):
            acc = acc + sh[k] * w_ref[kmax - k:kmax - k + 1, c0:c0 + width]
        outs.append(acc)
    return outs[0] if len(outs) == 1 else jnp.concatenate(outs, axis=0)


def _ada_kernel(c_ref, w_ref, b_ref, o_ref):
    ca = _silu(c_ref[...])
    c_hi, c_lo = _split_bf16(ca)
    w_hi, w_lo = _split_bf16(w_ref[...])
    acc = _dot(c_hi, w_hi) + _dot(c_hi, w_lo) + _dot(c_lo, w_hi)
    o_ref[...] = acc + b_ref[...]


def _ada_call(c, ada_w, ada_b):
    depth, d, n = ada_w.shape
    bsz = c.shape[0]
    tn = 1024
    return pl.pallas_call(
        _ada_kernel,
        grid=(depth, n // tn),
        in_specs=[
            pl.BlockSpec((bsz, d), lambda l, j: (0, 0)),
            pl.BlockSpec((None, d, tn), lambda l, j: (l, 0, j)),
            pl.BlockSpec((None, 1, tn), lambda l, j: (l, 0, j)),
        ],
        out_specs=pl.BlockSpec((None, bsz, tn), lambda l, j: (l, 0, j)),
        out_shape=jax.ShapeDtypeStruct((depth, bsz, n), F32),
        compiler_params=pltpu.CompilerParams(
            dimension_semantics=("arbitrary", "arbitrary"), vmem_limit_bytes=VMEM_LIMIT),
        name="ada_mod",
    )(c, ada_w, ada_b.reshape(depth, 1, n))


def _inproj_kernel(x_ref, mod_ref, g_ref, w_ref, wdt_ref, cw_ref, cb_ref, o_ref, dt_ref, halo_ref,
                   *, tm, d_inner, d_conv, x_natural):
    kmax = SSD_CONV_K - 1

    @pl.when(pl.program_id(1) == 0)
    def _():
        halo_ref[...] = jnp.zeros_like(halo_ref)

    x = _load_rows(x_ref, x_natural)
    h = (_rms(x) * g_ref[...] * (1.0 + mod_ref[1:2, :]) + mod_ref[0:1, :]).astype(BF16)
    dt_ref[...] = _dot(h, wdt_ref[...])

    sub = 256
    starts = list(range(0, w_ref.shape[1], sub))
    heavy = [c for c in starts if d_inner <= c < d_inner + d_conv]
    light = [c for c in starts if not d_inner <= c < d_inner + d_conv]
    order = []
    for n, c in enumerate(heavy):
        order.append(c)
        order.extend(light[n * len(light) // len(heavy):(n + 1) * len(light) // len(heavy)])
    assert sorted(order) == starts
    for c0 in order:
        cols = slice(c0, c0 + sub)
        pre = _dot(h, w_ref[:, cols])
        if c0 < d_inner:
            o_ref[:, cols] = _silu_of_double(pre).astype(BF16)
        elif c0 < d_inner + d_conv:
            cc = slice(c0 - d_inner, c0 - d_inner + sub)
            halo = halo_ref[:, cc]
            halo_ref[:, cc] = pre[tm - kmax * SUBLANE:tm]
            half_acc = _causal_conv(pre, halo, cw_ref, cc.start, sub, SSD_CONV_K) + 0.5 * cb_ref[:, cc]
            o_ref[:, cols] = _silu_of_double(half_acc).astype(BF16)
        else:
            o_ref[:, cols] = pre.astype(BF16)


def _inproj_call(x, mod3, g, w_main, w_dt, layer, conv_w, conv_b, tm, d_inner, x_natural):
    bsz, seq, d = x.shape
    n = w_main.shape[2]
    slab = lambda b, i: (layer, 0, 0)
    d_conv = conv_w.shape[1]
    kern = functools.partial(_inproj_kernel, tm=tm, d_inner=d_inner, d_conv=d_conv, x_natural=x_natural)
    const = lambda b, i: (0, 0)
    return pl.pallas_call(
        kern,
        grid=(bsz, seq // tm),
        in_specs=[
            pl.BlockSpec((None, tm, d), lambda b, i: (b, i, 0)),
            pl.BlockSpec((None, 6, d), lambda b, i: (b, 0, 0)),
            pl.BlockSpec((1, d), const),
            pl.BlockSpec((None, d, n), slab, pipeline_mode=pl.Buffered(1)),
            pl.BlockSpec((None, d, LANE), slab),
            pl.BlockSpec((SSD_CONV_K, d_conv), const),
            pl.BlockSpec((1, d_conv), const),
        ],
        out_specs=[
            pl.BlockSpec((None, tm, n), lambda b, i: (b, i, 0)),
            pl.BlockSpec((None, tm, LANE), lambda b, i: (b, i, 0)),
        ],
        out_shape=[
            jax.ShapeDtypeStruct((bsz, seq, n), BF16),
            jax.ShapeDtypeStruct((bsz, seq, LANE), F32),
        ],
        scratch_shapes=[
            pltpu.VMEM(((SSD_CONV_K - 1) * SUBLANE, d_conv), F32),
        ],
        compiler_params=pltpu.CompilerParams(
            dimension_semantics=("parallel", "arbitrary"), vmem_limit_bytes=VMEM_LIMIT),
        name="inproj",
    )(x, mod3, g, w_main, w_dt, conv_w, conv_b)


def _ssd_kernel(zs_ref, xc_ref, bc_ref, dtr_ref, dtb_ref, a_ref, dexp_ref, ng_ref, tril_ref, exp_ref, o_ref,
                state_ref, a2t_ref, y_ref, *, tc, d_inner):
    n_heads = d_inner // HEAD_DIM
    hpg = n_heads // N_GROUPS
    gw = hpg * HEAD_DIM

    @pl.when(pl.program_id(1) == 0)
    def _():
        state_ref[...] = jnp.zeros_like(state_ref)

    tok_r = _token_of_row(lax.broadcasted_iota(jnp.int32, (BLOCK, BLOCK), 0))
    tok_c = _token_of_row(lax.broadcasted_iota(jnp.int32, (BLOCK, BLOCK), 1))
    causal = tok_r >= tok_c
    low_half = lax.broadcasted_iota(jnp.int32, (BLOCK, LANE), 1) < HEAD_DIM
    c_off = N_GROUPS * D_STATE

    def chunk_body(ci, carry):
        r0 = pl.multiple_of(ci * BLOCK, BLOCK)
        rows = pl.ds(r0, BLOCK)

        dtv = dtr_ref[rows, :] + dtb_ref[...]
        dt = jnp.maximum(dtv, 0.0) + jnp.log(1.0 + jnp.exp(-jnp.abs(dtv)))
        adt_hi, adt_lo = _split_bf16(dt * a_ref[...])
        a_cs = _dot(tril_ref[...], jnp.concatenate([adt_hi, adt_lo], axis=0))
        a_last = a_cs[BLOCK - 1:BLOCK, :]
        a2 = a_cs * LOG2E
        a2t_ref[...] = a2.T

        factors = jnp.concatenate(
            [jnp.exp(a_last - a_cs), dt, jnp.broadcast_to(jnp.exp(a_last), (2 * SUBLANE, LANE))], axis=0)
        f_hi, f_lo = _split_bf16(factors)
        fac = _dot(jnp.concatenate([f_hi, f_lo], axis=1), exp_ref[...])
        dec_exp = fac[0:BLOCK, :]
        dt_exp = fac[BLOCK:2 * BLOCK, :]
        cd_exp = fac[2 * BLOCK:2 * BLOCK + 1, :]

        for g in range(N_GROUPS):
            gsl = slice(g * gw, (g + 1) * gw)
            bg = bc_ref[rows, g * D_STATE:(g + 1) * D_STATE]
            cgb = bc_ref[rows, c_off + g * D_STATE:c_off + (g + 1) * D_STATE]
            cg = cgb.astype(F32)
            scores = lax.dot_general(cgb, bg, (((1,), (1,)), ((), ())), preferred_element_type=F32)
            scores = jnp.where(causal, scores, 0.0)
            xg = xc_ref[rows, gsl].astype(F32)
            xdt = xg * dt_exp[:, gsl]
            hg = state_ref[:, gsl]
            for j in range(hpg // 2):
                lanes = slice(j * LANE, (j + 1) * LANE)
                rhs = jnp.concatenate([xdt[:, lanes].astype(BF16), hg[:, lanes].astype(BF16)], axis=0)
                outs = []
                for q in range(2):
                    r = g * hpg + 2 * j + q
                    a2_col = jnp.broadcast_to(a2[:, r:r + 1], (BLOCK, BLOCK))
                    decay = jnp.exp2(jnp.minimum(a2_col - a2t_ref[r:r + 1, :], 0.0))
                    m = scores * decay
                    ce = cg * jnp.exp2(a2_col)
                    lhs = jnp.concatenate([m.astype(BF16), ce.astype(BF16)], axis=1)
                    outs.append(_dot(lhs, rhs))
                y_pair = jnp.where(low_half, outs[0], outs[1])
                gl = slice(g * gw + j * LANE, g * gw + (j + 1) * LANE)
                y_ref[:, gl] = y_pair + dexp_ref[:, gl] * xg[:, lanes]
            xw = (xdt * dec_exp[:, gsl]).astype(BF16)
            upd = _dot(bg.astype(F32).T.astype(BF16), xw)
            state_ref[:, gsl] = cd_exp[:, gsl] * hg + upd

        y = y_ref[...] * zs_ref[rows, :].astype(F32)
        for g in range(N_GROUPS):
            gsl = slice(g * gw, (g + 1) * gw)
            o_ref[rows, gsl] = (_rms(y[:, gsl]) * ng_ref[:, gsl]).astype(BF16)
        return carry

    lax.fori_loop(0, tc // BLOCK, chunk_body, 0, unroll=2)


def _ssd_call(proj, dt_raw, dt_bias_row, a_row, d_exp, norm_g, tril2, expand2, tc, d_inner):
    bsz, seq, _ = proj.shape
    d_bc = 2 * N_GROUPS * D_STATE
    kern = functools.partial(_ssd_kernel, tc=tc, d_inner=d_inner)
    const = lambda b, s: (0, 0)
    return pl.pallas_call(
        kern,
        grid=(bsz, seq // tc),
        in_specs=[
            pl.BlockSpec((None, tc, d_inner), lambda b, s: (b, s, 0)),
            pl.BlockSpec((None, tc, d_inner), lambda b, s: (b, s, 1)),
            pl.BlockSpec((None, tc, d_bc), lambda b, s: (b, s, 2 * d_inner // d_bc)),
            pl.BlockSpec((None, tc, LANE), lambda b, s: (b, s, 0)),
            pl.BlockSpec((1, LANE), const),
            pl.BlockSpec((1, LANE), const),
            pl.BlockSpec((1, d_inner), const),
            pl.BlockSpec((1, d_inner), const),
            pl.BlockSpec((BLOCK, 2 * BLOCK), const),
            pl.BlockSpec((2 * LANE, d_inner), const),
        ],
        out_specs=pl.BlockSpec((None, tc, d_inner), lambda b, s: (b, s, 0)),
        out_shape=jax.ShapeDtypeStruct((bsz, seq, d_inner), BF16),
        scratch_shapes=[
            pltpu.VMEM((D_STATE, d_inner), F32),
            pltpu.VMEM((LANE, BLOCK), F32),
            pltpu.VMEM((BLOCK, d_inner), F32),
        ],
        compiler_params=pltpu.CompilerParams(
            dimension_semantics=("parallel", "arbitrary"), vmem_limit_bytes=VMEM_LIMIT),
        name="ssd",
    )(proj, proj, proj, dt_raw, dt_bias_row, a_row, d_exp, norm_g, tril2, expand2)


def _mix_kernel(x_ref, mod_ref, y_ref, scb_ref, scc_ref, sch_ref, gs_ref, gc_ref,
                wso_ref, scw_ref, wsc_ref, wo_ref, pg_ref, o_ref, halo_ref, *, tm, x_natural):
    d = x_ref.shape[1]

    @pl.when(pl.program_id(1) == 0)
    def _():
        halo_ref[...] = jnp.zeros_like(halo_ref)

    v = scc_ref[...].astype(F32) * sch_ref[...].astype(F32)
    halo = halo_ref[...]
    halo_ref[...] = v[tm - (SC_CONV_K - 1) * SUBLANE:tm]
    conv = _causal_conv(v, halo, scw_ref, 0, d, SC_CONV_K)
    u = (scb_ref[...].astype(F32) * conv).astype(BF16)
    y_sc = _dot(u, wsc_ref[...])
    y_ssd = _dot(y_ref[...], wso_ref[...])
    merged = (_sigmoid_of_double(gs_ref[...].astype(F32)) * y_ssd
              + _sigmoid_of_double(gc_ref[...].astype(F32)) * y_sc)
    mix = _dot(merged.astype(BF16), wo_ref[...])
    o_ref[...] = _load_rows(x_ref, x_natural) + mod_ref[2:3, :] * (_rms(mix) * pg_ref[...])


def _mix_call(x, mod3, y, proj, w_ssd_out, sc_conv_w, w_sc_out, w_o, layer, post_g, tm, x_natural):
    slab = lambda b, s: (layer, 0, 0)
    bsz, seq, d = x.shape
    d_inner = y.shape[2]
    kern = functools.partial(_mix_kernel, tm=tm, x_natural=x_natural)
    const = lambda b, s: (0, 0)
    base = (2 * d_inner + 2 * N_GROUPS * D_STATE) // d

    def pblock(k):
        return pl.BlockSpec((None, tm, d), lambda b, s, k=k: (b, s, base + k))

    return pl.pallas_call(
        kern,
        grid=(bsz, seq // tm),
        in_specs=[
            pl.BlockSpec((None, tm, d), lambda b, s: (b, s, 0)),
            pl.BlockSpec((None, 6, d), lambda b, s: (b, 0, 0)),
            pl.BlockSpec((None, tm, d_inner), lambda b, s: (b, s, 0)),
            pblock(0), pblock(1), pblock(2), pblock(3), pblock(4),
            pl.BlockSpec((None, d_inner, d), slab),
            pl.BlockSpec((SC_CONV_K, d), const),
            pl.BlockSpec((None, d, d), slab),
            pl.BlockSpec((None, d, d), slab),
            pl.BlockSpec((1, d), const),
        ],
        out_specs=pl.BlockSpec((None, tm, d), lambda b, s: (b, s, 0)),
        out_shape=jax.ShapeDtypeStruct((bsz, seq, d), F32),
        scratch_shapes=[pltpu.VMEM(((SC_CONV_K - 1) * SUBLANE, d), F32)],
        compiler_params=pltpu.CompilerParams(
            dimension_semantics=("parallel", "arbitrary"), vmem_limit_bytes=VMEM_LIMIT),
        name="mix",
    )(x, mod3, y, proj, proj, proj, proj, proj, w_ssd_out, sc_conv_w, w_sc_out, w_o, post_g)


def _ffn_kernel(x_ref, mod_ref, pre_ref, wup_ref, cw_ref, cb_ref, wdn_ref, pg_ref, o_ref,
                halo_ref, act_ref, *, tm, d_ff, cw, out_natural):
    @pl.when(pl.program_id(1) == 0)
    def _():
        halo_ref[...] = jnp.zeros_like(halo_ref)

    x = x_ref[...]
    h = (_rms(x) * pre_ref[...] * (1.0 + mod_ref[4:5, :]) + mod_ref[3:4, :]).astype(BF16)

    def conv_cols(c0, bias_scale):
        pre = _dot(h, wup_ref[:, c0:c0 + cw])
        halo = halo_ref[:, c0:c0 + cw]
        halo_ref[:, c0:c0 + cw] = pre[tm - (FFN_CONV_K - 1) * SUBLANE:tm]
        bias = cb_ref[:, c0:c0 + cw]
        if bias_scale != 1.0:
            bias = bias_scale * bias
        return _causal_conv(pre, halo, cw_ref, c0, cw, FFN_CONV_K) + bias

    for c0 in range(0, d_ff, cw):
        half_gate = conv_cols(c0, 0.5)
        val = conv_cols(d_ff + c0, 1.0)
        act_ref[:, c0:c0 + cw] = (_silu_of_double(half_gate) * val).astype(BF16)
    f = _dot(act_ref[...], wdn_ref[...])
    _store_rows(o_ref, x + mod_ref[5:6, :] * (_rms(f) * pg_ref[...]), out_natural)


def _ffn_call(x, mod3, pre_g, w_up, conv_w, conv_b, w_down, layer, post_g, tm, cw, out_natural):
    bsz, seq, d = x.shape
    d_ff = w_down.shape[1]
    slab = lambda b, s: (layer, 0, 0)
    kern = functools.partial(_ffn_kernel, tm=tm, d_ff=d_ff, cw=cw, out_natural=out_natural)
    const = lambda b, s: (0, 0)
    return pl.pallas_call(
        kern,
        grid=(bsz, seq // tm),
        in_specs=[
            pl.BlockSpec((None, tm, d), lambda b, s: (b, s, 0)),
            pl.BlockSpec((None, 6, d), lambda b, s: (b, 0, 0)),
            pl.BlockSpec((1, d), const),
            pl.BlockSpec((None, d, 2 * d_ff), slab, pipeline_mode=pl.Buffered(1)),
            pl.BlockSpec((FFN_CONV_K, 2 * d_ff), const),
            pl.BlockSpec((1, 2 * d_ff), const),
            pl.BlockSpec((None, d_ff, d), slab, pipeline_mode=pl.Buffered(1)),
            pl.BlockSpec((1, d), const),
        ],
        out_specs=pl.BlockSpec((None, tm, d), lambda b, s: (b, s, 0)),
        out_shape=jax.ShapeDtypeStruct((bsz, seq, d), F32),
        scratch_shapes=[
            pltpu.VMEM(((FFN_CONV_K - 1) * SUBLANE, 2 * d_ff), F32),
            pltpu.VMEM((tm, d_ff), BF16),
        ],
        compiler_params=pltpu.CompilerParams(
            dimension_semantics=("parallel", "arbitrary"), vmem_limit_bytes=VMEM_LIMIT),
        name="ffn",
    )(x, mod3, pre_g, w_up, conv_w, conv_b, w_down, post_g)


def _pick_tile(n, pref):
    t = min(n, pref)
    while n % t:
        t //= 2
    return t


def _to_block_order(x):
    b, s, d = x.shape
    return x.reshape(b, s // BLOCK, SUBLANE, SEG, d).transpose(0, 1, 3, 2, 4).reshape(b, s, d)


def _from_block_order(x):
    b, s, d = x.shape
    return x.reshape(b, s // BLOCK, SEG, SUBLANE, d).transpose(0, 1, 3, 2, 4).reshape(b, s, d)


def _win_prep_kernel(w_ref, main_ref, dt_ref, *, o_dt, n_heads, o_gate):
    o_sc = o_dt + n_heads
    n_sc = o_gate - o_sc
    n_all = w_ref.shape[1]
    main_ref[:, 0:o_dt] = (0.5 * w_ref[:, 0:o_dt]).astype(BF16)
    main_ref[:, o_dt:o_dt + n_sc] = w_ref[:, o_sc:o_gate].astype(BF16)
    main_ref[:, o_dt + n_sc:] = (0.5 * w_ref[:, o_gate:n_all]).astype(BF16)
    pad = jnp.zeros((w_ref.shape[0], LANE - n_heads), F32)
    dt_ref[...] = jnp.concatenate([w_ref[:, o_dt:o_sc], pad], axis=1).astype(BF16)


def _win_prep_call(w_in, o_dt, n_heads, d):
    depth, k, n_all = w_in.shape
    o_gate = n_all - 2 * d
    n_main = n_all - n_heads
    rb = 128
    kern = functools.partial(_win_prep_kernel, o_dt=o_dt, n_heads=n_heads, o_gate=o_gate)
    return pl.pallas_call(
        kern,
        grid=(depth, k // rb),
        in_specs=[pl.BlockSpec((None, rb, n_all), lambda l, r: (l, r, 0))],
        out_specs=[
            pl.BlockSpec((None, rb, n_main), lambda l, r: (l, r, 0)),
            pl.BlockSpec((None, rb, LANE), lambda l, r: (l, r, 0)),
        ],
        out_shape=[
            jax.ShapeDtypeStruct((depth, k, n_main), BF16),
            jax.ShapeDtypeStruct((depth, k, LANE), BF16),
        ],
        compiler_params=pltpu.CompilerParams(
            dimension_semantics=("arbitrary", "arbitrary"), vmem_limit_bytes=VMEM_LIMIT),
        name="win_prep",
    )(w_in)


def kernel(x, c, ada_w, ada_b, mix_pre_g, mix_post_g, w_in, ssd_conv_w, ssd_conv_b, ssd_dt_bias, ssd_a_log,
           ssd_d, ssd_norm_g, w_ssd_out, sc_conv_w, w_sc_out, w_o, ffn_pre_g, ffn_post_g, w_up, ffn_conv_w,
           ffn_conv_b, w_down):
    bsz, seq, d = x.shape
    depth = ada_w.shape[0]
    d_inner = ssd_norm_g.shape[1]
    n_heads = ssd_dt_bias.shape[1]
    d_conv = ssd_conv_w.shape[2]
    d_ff = w_down.shape[1]
    assert d_inner == n_heads * HEAD_DIM and d_conv == d_inner + 2 * N_GROUPS * D_STATE
    assert n_heads <= LANE and seq % BLOCK == 0

    mod = _ada_call(c, ada_w, ada_b).reshape(depth, bsz, 6, d)

    tok = _token_of_row(jnp.arange(BLOCK))
    tril = (tok[:, None] >= tok[None, :]).astype(BF16)
    tril2 = jnp.concatenate([tril, tril], axis=1)
    head_of_lane = jnp.arange(d_inner) // HEAD_DIM
    expand = (jnp.arange(LANE)[:, None] == head_of_lane[None, :]).astype(BF16)
    expand2 = jnp.concatenate([expand, expand], axis=0)

    tc = _pick_tile(seq, 512)
    tm = _pick_tile(seq, 512)

    w_main_all, w_dt_all = _win_prep_call(w_in, d_inner + d_conv, n_heads, d)
    up_scale = jnp.where(jnp.arange(2 * d_ff) < d_ff, 0.5, 1.0).astype(F32)
    w_up_all = (w_up * up_scale).astype(BF16)
    w_down_all = w_down.astype(BF16)
    w_ssd_out_all = w_ssd_out.astype(BF16)
    w_sc_out_all = w_sc_out.astype(BF16)
    w_o_all = w_o.astype(BF16)

    x = _to_block_order(x)
    for i in range(depth):
        pad_h = (0, LANE - n_heads)
        dt_bias_row = jnp.pad(ssd_dt_bias[i], pad_h).reshape(1, LANE)
        a_row = jnp.pad(-jnp.exp(ssd_a_log[i]), pad_h).reshape(1, LANE)
        d_exp = jnp.repeat(ssd_d[i], HEAD_DIM).reshape(1, d_inner)

        proj, dt_raw = _inproj_call(x, mod[i], mix_pre_g[i].reshape(1, d), w_main_all, w_dt_all, i, ssd_conv_w[i],
                                    ssd_conv_b[i].reshape(1, d_conv), tm, d_inner, False)
        y = _ssd_call(proj, dt_raw, dt_bias_row, a_row, d_exp, ssd_norm_g[i].reshape(1, d_inner), tril2, expand2,
                      tc, d_inner)
        x = _mix_call(x, mod[i], y, proj, w_ssd_out_all, sc_conv_w[i], w_sc_out_all, w_o_all, i,
                      mix_post_g[i].reshape(1, d), tm, False)
        x = _ffn_call(x, mod[i], ffn_pre_g[i].reshape(1, d), w_up_all, ffn_conv_w[i],
                      ffn_conv_b[i].reshape(1, 2 * d_ff), w_down_all, i, ffn_post_g[i].reshape(1, d),
                      tm, 256, False)
    return _from_block_order(x)
```

```python
import functools
import math

import jax
import jax.numpy as jnp
from jax import lax
from jax.experimental import pallas as pl
from jax.experimental.pallas import tpu as pltpu

F32 = jnp.float32
BF16 = jnp.bfloat16

EPS = 1e-6
LANE = 128
SUBLANE = 8
VMEM_LIMIT = 56 * 1024 * 1024

HEAD_DIM = 64
N_GROUPS = 4
D_STATE = 128
BLOCK = 128
SEG = BLOCK // SUBLANE
SSD_CONV_K = 4
SC_CONV_K = 3
FFN_CONV_K = 3
LOG2E = math.log2(math.e)


def _dot(a, b):
    return jnp.dot(a, b, preferred_element_type=F32)


def _sigmoid(v):
    return 1.0 / (1.0 + jnp.exp(-v))


def _silu(v):
    return v * _sigmoid(v)


def _silu_of_double(hv):
    return hv + hv * jnp.tanh(hv)


def _sigmoid_of_double(hv):
    return 0.5 + 0.5 * jnp.tanh(hv)


def _split_bf16(v):
    hi = v.astype(BF16)
    lo = (v - hi.astype(F32)).astype(BF16)
    return hi, lo


def _rms(v):
    return v * lax.rsqrt(jnp.mean(v * v, axis=-1, keepdims=True) + EPS)


def _token_of_row(r):
    return (r & (SUBLANE - 1)) * SEG + (r >> 3)


def _block_shifts(cur, prev_tail, kmax):
    n = cur.shape[0]
    tail = cur[n - kmax * SUBLANE:n]
    first_sublane = lax.broadcasted_iota(jnp.int32, (SUBLANE, cur.shape[1]), 0) == 0
    wrapped = []
    for m in range(kmax):
        rows = slice(m * SUBLANE, (m + 1) * SUBLANE)
        wrapped.append(jnp.where(first_sublane, pltpu.roll(prev_tail[rows], 1, 0), pltpu.roll(tail[rows], 1, 0)))
    out = [cur]
    for k in range(1, kmax + 1):
        out.append(jnp.concatenate(wrapped[kmax - k:] + [cur[0:n - k * SUBLANE]], axis=0))
    return out


def _causal_conv(val, halo, w_ref, c0, width, kw):
    kmax = kw - 1
    outs = []
    for b0 in range(0, val.shape[0], BLOCK):
        cur = val[b0:b0 + BLOCK]
        prev_tail = halo if b0 == 0 else val[b0 - kmax * SUBLANE:b0]
        sh = _block_shifts(cur, prev_tail, kmax)
        acc = sh[0] * w_ref[kmax:kmax + 1, c0:c0 + width]
        for k in range(1, kw):
            acc = acc + sh[k] * w_ref[kmax - k:kmax - k + 1, c0:c0 + width]
        outs.append(acc)
    return outs[0] if len(outs) == 1 else jnp.concatenate(outs, axis=0)


def _ada_kernel(c_ref, w_ref, b_ref, o_ref):
    ca = _silu(c_ref[...])
    c_hi, c_lo = _split_bf16(ca)
    w_hi, w_lo = _split_bf16(w_ref[...])
    acc = _dot(c_hi, w_hi) + _dot(c_hi, w_lo) + _dot(c_lo, w_hi)
    o_ref[...] = acc + b_ref[...]


def _ada_call(c, ada_w, ada_b):
    depth, d, n = ada_w.shape
    bsz = c.shape[0]
    tn = 1024
    return pl.pallas_call(
        _ada_kernel,
        grid=(depth, n // tn),
        in_specs=[
            pl.BlockSpec((bsz, d), lambda l, j: (0, 0)),
            pl.BlockSpec((None, d, tn), lambda l, j: (l, 0, j)),
            pl.BlockSpec((None, 1, tn), lambda l, j: (l, 0, j)),
        ],
        out_specs=pl.BlockSpec((None, bsz, tn), lambda l, j: (l, 0, j)),
        out_shape=jax.ShapeDtypeStruct((depth, bsz, n), F32),
        compiler_params=pltpu.CompilerParams(
            dimension_semantics=("arbitrary", "arbitrary"), vmem_limit_bytes=VMEM_LIMIT),
        name="ada_mod",
    )(c, ada_w, ada_b.reshape(depth, 1, n))


def _inproj_kernel(x_ref, mod_ref, g_ref, w_ref, wdt_ref, cw_ref, cb_ref, o_ref, dt_ref, halo_ref,
                   *, tm, d_inner, d_conv):
    kmax = SSD_CONV_K - 1

    @pl.when(pl.program_id(1) == 0)
    def _():
        halo_ref[...] = jnp.zeros_like(halo_ref)

    h = (_rms(x_ref[...]) * g_ref[...] * (1.0 + mod_ref[1:2, :]) + mod_ref[0:1, :]).astype(BF16)
    dt_ref[...] = _dot(h, wdt_ref[...])

    sub = 256
    starts = list(range(0, w_ref.shape[1], sub))
    heavy = [c for c in starts if d_inner <= c < d_inner + d_conv]
    light = [c for c in starts if not d_inner <= c < d_inner + d_conv]
    order = []
    for n, c in enumerate(heavy):
        order.append(c)
        order.extend(light[n * len(light) // len(heavy):(n + 1) * len(light) // len(heavy)])
    assert sorted(order) == starts
    for c0 in order:
        cols = slice(c0, c0 + sub)
        pre = _dot(h, w_ref[:, cols])
        if c0 < d_inner:
            o_ref[:, cols] = _silu_of_double(pre).astype(BF16)
        elif c0 < d_inner + d_conv:
            cc = slice(c0 - d_inner, c0 - d_inner + sub)
            halo = halo_ref[:, cc]
            halo_ref[:, cc] = pre[tm - kmax * SUBLANE:tm]
            half_acc = _causal_conv(pre, halo, cw_ref, cc.start, sub, SSD_CONV_K) + 0.5 * cb_ref[:, cc]
            o_ref[:, cols] = _silu_of_double(half_acc).astype(BF16)
        else:
            o_ref[:, cols] = pre.astype(BF16)


def _inproj_call(x, mod3, g, w_main, w_dt, layer, conv_w, conv_b, tm, d_inner):
    bsz, seq, d = x.shape
    n = w_main.shape[2]
    slab = lambda b, i: (layer, 0, 0)
    d_conv = conv_w.shape[1]
    kern = functools.partial(_inproj_kernel, tm=tm, d_inner=d_inner, d_conv=d_conv)
    const = lambda b, i: (0, 0)
    return pl.pallas_call(
        kern,
        grid=(bsz, seq // tm),
        in_specs=[
            pl.BlockSpec((None, tm, d), lambda b, i: (b, i, 0)),
            pl.BlockSpec((None, 6, d), lambda b, i: (b, 0, 0)),
            pl.BlockSpec((1, d), const),
            pl.BlockSpec((None, d, n), slab, pipeline_mode=pl.Buffered(1)),
            pl.BlockSpec((None, d, LANE), slab),
            pl.BlockSpec((SSD_CONV_K, d_conv), const),
            pl.BlockSpec((1, d_conv), const),
        ],
        out_specs=[
            pl.BlockSpec((None, tm, n), lambda b, i: (b, i, 0)),
            pl.BlockSpec((None, tm, LANE), lambda b, i: (b, i, 0)),
        ],
        out_shape=[
            jax.ShapeDtypeStruct((bsz, seq, n), BF16),
            jax.ShapeDtypeStruct((bsz, seq, LANE), F32),
        ],
        scratch_shapes=[
            pltpu.VMEM(((SSD_CONV_K - 1) * SUBLANE, d_conv), F32),
        ],
        compiler_params=pltpu.CompilerParams(
            dimension_semantics=("parallel", "arbitrary"), vmem_limit_bytes=VMEM_LIMIT),
        name="inproj",
    )(x, mod3, g, w_main, w_dt, conv_w, conv_b)


def _ssd_chunk(r0, zs_ref, xc_ref, bc_ref, dtr_ref, dtb_ref, a_ref, dexp_ref, ng_ref, tril_ref, exp_ref, y_ref,
               state_ref, a2t_ref, causal, low_half, d_inner, after_group):
    n_heads = d_inner // HEAD_DIM
    hpg = n_heads // N_GROUPS
    gw = hpg * HEAD_DIM
    c_off = N_GROUPS * D_STATE
    rows = pl.ds(r0, BLOCK)

    dtv = dtr_ref[rows, :] + dtb_ref[...]
    dt = jnp.maximum(dtv, 0.0) + jnp.log(1.0 + jnp.exp(-jnp.abs(dtv)))
    adt_hi, adt_lo = _split_bf16(dt * a_ref[...])
    a_cs = _dot(tril_ref[...], jnp.concatenate([adt_hi, adt_lo], axis=0))
    a_last = a_cs[BLOCK - 1:BLOCK, :]
    a2 = a_cs * LOG2E
    a2t_ref[...] = a2.T

    factors = jnp.concatenate(
        [jnp.exp(a_last - a_cs), dt, jnp.broadcast_to(jnp.exp(a_last), (2 * SUBLANE, LANE))], axis=0)
    f_hi, f_lo = _split_bf16(factors)
    fac = _dot(jnp.concatenate([f_hi, f_lo], axis=1), exp_ref[...])
    dec_exp = fac[0:BLOCK, :]
    dt_exp = fac[BLOCK:2 * BLOCK, :]
    cd_exp = fac[2 * BLOCK:2 * BLOCK + 1, :]

    for g in range(N_GROUPS):
        gsl = slice(g * gw, (g + 1) * gw)
        bg = bc_ref[rows, g * D_STATE:(g + 1) * D_STATE]
        cgb = bc_ref[rows, c_off + g * D_STATE:c_off + (g + 1) * D_STATE]
        cg = cgb.astype(F32)
        scores = lax.dot_general(cgb, bg, (((1,), (1,)), ((), ())), preferred_element_type=F32)
        scores = jnp.where(causal, scores, 0.0)
        xg = xc_ref[rows, gsl].astype(F32)
        xdt = xg * dt_exp[:, gsl]
        hg = state_ref[:, gsl]
        y_pairs = []
        for j in range(hpg // 2):
            lanes = slice(j * LANE, (j + 1) * LANE)
            rhs = jnp.concatenate([xdt[:, lanes].astype(BF16), hg[:, lanes].astype(BF16)], axis=0)
            outs = []
            for q in range(2):
                r = g * hpg + 2 * j + q
                a2_col = jnp.broadcast_to(a2[:, r:r + 1], (BLOCK, BLOCK))
                decay = jnp.exp2(jnp.minimum(a2_col - a2t_ref[r:r + 1, :], 0.0))
                m = scores * decay
                ce = cg * jnp.exp2(a2_col)
                lhs = jnp.concatenate([m.astype(BF16), ce.astype(BF16)], axis=1)
                outs.append(_dot(lhs, rhs))
            y_pair = jnp.where(low_half, outs[0], outs[1])
            gl = slice(g * gw + j * LANE, g * gw + (j + 1) * LANE)
            y_pairs.append(y_pair + dexp_ref[:, gl] * xg[:, lanes])
        xw = (xdt * dec_exp[:, gsl]).astype(BF16)
        upd = _dot(bg.astype(F32).T.astype(BF16), xw)
        state_ref[:, gsl] = cd_exp[:, gsl] * hg + upd
        yg = jnp.concatenate(y_pairs, axis=1) * zs_ref[rows, gsl].astype(F32)
        y_ref[rows, gsl] = (_rms(yg) * ng_ref[:, gsl]).astype(BF16)
        after_group(g)


def _ssdmix_kernel(zs_ref, xc_ref, bc_ref, dtr_ref, dtb_ref, a_ref, dexp_ref, ng_ref, tril_ref, exp_ref,
                   x_ref, mod_ref, scb_ref, scc_ref, sch_ref, gs_ref, gc_ref,
                   wso_ref, scw_ref, wsc_ref, wo_ref, pg_ref, o_ref,
                   state_ref, a2t_ref, ycur_ref, yprev_ref, halo_ref, *, tc, d_inner, n_s):
    g = pl.program_id(0)
    d = x_ref.shape[1]

    @pl.when(g % n_s == 0)
    def _():
        state_ref[...] = jnp.zeros_like(state_ref)

    @pl.when(jnp.maximum(g - 1, 0) % n_s == 0)
    def _():
        halo_ref[...] = jnp.zeros_like(halo_ref)

    @pl.when(g == 0)
    def _():
        ycur_ref[...] = jnp.zeros_like(ycur_ref)

    yprev_ref[...] = ycur_ref[...]

    tok_r = _token_of_row(lax.broadcasted_iota(jnp.int32, (BLOCK, BLOCK), 0))
    tok_c = _token_of_row(lax.broadcasted_iota(jnp.int32, (BLOCK, BLOCK), 1))
    causal = tok_r >= tok_c
    low_half = lax.broadcasted_iota(jnp.int32, (BLOCK, LANE), 1) < HEAD_DIM

    sub = 256
    col_tiles = [slice(c0, c0 + sub) for c0 in range(0, d, sub)]
    vals = {"y_ssd": [], "y_sc": [], "mix": []}

    def ssd_out_slice(cols):
        return lambda: vals["y_ssd"].append(_dot(yprev_ref[...], wso_ref[:, cols]))

    def short_conv():
        v = scc_ref[...].astype(F32) * sch_ref[...].astype(F32)
        halo = halo_ref[...]
        halo_ref[...] = v[tc - (SC_CONV_K - 1) * SUBLANE:tc]
        conv = _causal_conv(v, halo, scw_ref, 0, d, SC_CONV_K)
        vals["u"] = (scb_ref[...].astype(F32) * conv).astype(BF16)

    def sc_out_slice(cols):
        return lambda: vals["y_sc"].append(_dot(vals["u"], wsc_ref[:, cols]))

    def merge():
        y_ssd = jnp.concatenate(vals["y_ssd"], axis=1)
        y_sc = jnp.concatenate(vals["y_sc"], axis=1)
        vals["merged"] = (_sigmoid_of_double(gs_ref[...].astype(F32)) * y_ssd
                          + _sigmoid_of_double(gc_ref[...].astype(F32)) * y_sc).astype(BF16)

    def w_o_slice(cols):
        return lambda: vals["mix"].append(_dot(vals["merged"], wo_ref[:, cols]))

    def residual():
        mix = jnp.concatenate(vals["mix"], axis=1)
        o_ref[...] = x_ref[...] + mod_ref[2:3, :] * (_rms(mix) * pg_ref[...])

    pieces = ([ssd_out_slice(c) for c in col_tiles] + [short_conv] + [sc_out_slice(c) for c in col_tiles]
              + [merge] + [w_o_slice(c) for c in col_tiles] + [residual])
    n_chunks = tc // BLOCK
    n_slots = n_chunks * N_GROUPS
    done = [0]

    def run_pieces(slot):
        upto = len(pieces) * (slot + 1) // n_slots
        while done[0] < upto:
            pieces[done[0]]()
            done[0] += 1

    for c in range(n_chunks):
        _ssd_chunk(c * BLOCK, zs_ref, xc_ref, bc_ref, dtr_ref, dtb_ref, a_ref, dexp_ref, ng_ref, tril_ref, exp_ref,
                   ycur_ref, state_ref, a2t_ref, causal, low_half, d_inner,
                   lambda g, c=c: run_pieces(c * N_GROUPS + g))


def _ssdmix_call(proj, dt_raw, dt_bias_row, a_row, d_exp, norm_g, tril2, expand2, x, mod3,
                 w_ssd_out, sc_conv_w, w_sc_out, w_o, layer, post_g, tc, d_inner):
    bsz, seq, d = x.shape
    d_bc = 2 * N_GROUPS * D_STATE
    n_s = seq // tc
    n_tiles = bsz * n_s
    kern = functools.partial(_ssdmix_kernel, tc=tc, d_inner=d_inner, n_s=n_s)
    const = lambda g: (0, 0)
    slab = lambda g: (layer, 0, 0)
    base = (2 * d_inner + d_bc) // d

    def scan_block(col):
        def index(g):
            t = jnp.minimum(g, n_tiles - 1)
            return (t // n_s, t % n_s, col)
        return index

    def mix_block(col):
        def index(g):
            t = jnp.maximum(g - 1, 0)
            return (t // n_s, t % n_s, col)
        return index

    def mix_mod(g):
        return (jnp.maximum(g - 1, 0) // n_s, 0, 0)

    one = pl.Buffered(1)
    return pl.pallas_call(
        kern,
        grid=(n_tiles + 1,),
        in_specs=[
            pl.BlockSpec((None, tc, d_inner), scan_block(0)),
            pl.BlockSpec((None, tc, d_inner), scan_block(1)),
            pl.BlockSpec((None, tc, d_bc), scan_block(2 * d_inner // d_bc)),
            pl.BlockSpec((None, tc, LANE), scan_block(0)),
            pl.BlockSpec((1, LANE), const),
            pl.BlockSpec((1, LANE), const),
            pl.BlockSpec((1, d_inner), const),
            pl.BlockSpec((1, d_inner), const),
            pl.BlockSpec((BLOCK, 2 * BLOCK), const),
            pl.BlockSpec((2 * LANE, d_inner), const),
            pl.BlockSpec((None, tc, d), mix_block(0)),
            pl.BlockSpec((None, 6, d), mix_mod),
            pl.BlockSpec((None, tc, d), mix_block(base)),
            pl.BlockSpec((None, tc, d), mix_block(base + 1)),
            pl.BlockSpec((None, tc, d), mix_block(base + 2)),
            pl.BlockSpec((None, tc, d), mix_block(base + 3)),
            pl.BlockSpec((None, tc, d), mix_block(base + 4)),
            pl.BlockSpec((None, d_inner, d), slab, pipeline_mode=one),
            pl.BlockSpec((SC_CONV_K, d), const),
            pl.BlockSpec((None, d, d), slab, pipeline_mode=one),
            pl.BlockSpec((None, d, d), slab, pipeline_mode=one),
            pl.BlockSpec((1, d), const),
        ],
        out_specs=pl.BlockSpec((None, tc, d), mix_block(0)),
        out_shape=jax.ShapeDtypeStruct((bsz, seq, d), F32),
        scratch_shapes=[
            pltpu.VMEM((D_STATE, d_inner), F32),
            pltpu.VMEM((LANE, BLOCK), F32),
            pltpu.VMEM((tc, d_inner), BF16),
            pltpu.VMEM((tc, d_inner), BF16),
            pltpu.VMEM(((SC_CONV_K - 1) * SUBLANE, d), F32),
        ],
        compiler_params=pltpu.CompilerParams(
            dimension_semantics=("arbitrary",), vmem_limit_bytes=VMEM_LIMIT),
        name="ssdmix",
    )(proj, proj, proj, dt_raw, dt_bias_row, a_row, d_exp, norm_g, tril2, expand2,
      x, mod3, proj, proj, proj, proj, proj, w_ssd_out, sc_conv_w, w_sc_out, w_o, post_g)


def _ffn_kernel(x_ref, mod_ref, pre_ref, wup_ref, cw_ref, cb_ref, wdn_ref, pg_ref, o_ref,
                halo_ref, act_ref, *, tm, d_ff, cw):
    @pl.when(pl.program_id(1) == 0)
    def _():
        halo_ref[...] = jnp.zeros_like(halo_ref)

    x = x_ref[...]
    h = (_rms(x) * pre_ref[...] * (1.0 + mod_ref[4:5, :]) + mod_ref[3:4, :]).astype(BF16)

    def conv_cols(c0, bias_scale):
        pre = _dot(h, wup_ref[:, c0:c0 + cw])
        halo = halo_ref[:, c0:c0 + cw]
        halo_ref[:, c0:c0 + cw] = pre[tm - (FFN_CONV_K - 1) * SUBLANE:tm]
        bias = cb_ref[:, c0:c0 + cw]
        if bias_scale != 1.0:
            bias = bias_scale * bias
        return _causal_conv(pre, halo, cw_ref, c0, cw, FFN_CONV_K) + bias

    for c0 in range(0, d_ff, cw):
        half_gate = conv_cols(c0, 0.5)
        val = conv_cols(d_ff + c0, 1.0)
        act_ref[:, c0:c0 + cw] = (_silu_of_double(half_gate) * val).astype(BF16)
    f = _dot(act_ref[...], wdn_ref[...])
    o_ref[...] = x + mod_ref[5:6, :] * (_rms(f) * pg_ref[...])


def _ffn_call(x, mod3, pre_g, w_up, conv_w, conv_b, w_down, layer, post_g, tm, cw):
    bsz, seq, d = x.shape
    d_ff = w_down.shape[1]
    slab = lambda b, s: (layer, 0, 0)
    kern = functools.partial(_ffn_kernel, tm=tm, d_ff=d_ff, cw=cw)
    const = lambda b, s: (0, 0)
    return pl.pallas_call(
        kern,
        grid=(bsz, seq // tm),
        in_specs=[
            pl.BlockSpec((None, tm, d), lambda b, s: (b, s, 0)),
            pl.BlockSpec((None, 6, d), lambda b, s: (b, 0, 0)),
            pl.BlockSpec((1, d), const),
            pl.BlockSpec((None, d, 2 * d_ff), slab, pipeline_mode=pl.Buffered(1)),
            pl.BlockSpec((FFN_CONV_K, 2 * d_ff), const),
            pl.BlockSpec((1, 2 * d_ff), const),
            pl.BlockSpec((None, d_ff, d), slab, pipeline_mode=pl.Buffered(1)),
            pl.BlockSpec((1, d), const),
        ],
        out_specs=pl.BlockSpec((None, tm, d), lambda b, s: (b, s, 0)),
        out_shape=jax.ShapeDtypeStruct((bsz, seq, d), F32),
        scratch_shapes=[
            pltpu.VMEM(((FFN_CONV_K - 1) * SUBLANE, 2 * d_ff), F32),
            pltpu.VMEM((tm, d_ff), BF16),
        ],
        compiler_params=pltpu.CompilerParams(
            dimension_semantics=("parallel", "arbitrary"), vmem_limit_bytes=VMEM_LIMIT),
        name="ffn",
    )(x, mod3, pre_g, w_up, conv_w, conv_b, w_down, post_g)


def _pick_tile(n, pref):
    t = min(n, pref)
    while n % t:
        t //= 2
    return t


def _to_block_order(x):
    b, s, d = x.shape
    return x.reshape(b, s // BLOCK, SUBLANE, SEG, d).transpose(0, 1, 3, 2, 4).reshape(b, s, d)


def _from_block_order(x):
    b, s, d = x.shape
    return x.reshape(b, s // BLOCK, SEG, SUBLANE, d).transpose(0, 1, 3, 2, 4).reshape(b, s, d)


def _win_prep_kernel(w_ref, main_ref, dt_ref, *, o_dt, n_heads, o_gate):
    o_sc = o_dt + n_heads
    n_sc = o_gate - o_sc
    n_all = w_ref.shape[1]
    main_ref[:, 0:o_dt] = (0.5 * w_ref[:, 0:o_dt]).astype(BF16)
    main_ref[:, o_dt:o_dt + n_sc] = w_ref[:, o_sc:o_gate].astype(BF16)
    main_ref[:, o_dt + n_sc:] = (0.5 * w_ref[:, o_gate:n_all]).astype(BF16)
    pad = jnp.zeros((w_ref.shape[0], LANE - n_heads), F32)
    dt_ref[...] = jnp.concatenate([w_ref[:, o_dt:o_sc], pad], axis=1).astype(BF16)


def _win_prep_call(w_in, o_dt, n_heads, d):
    depth, k, n_all = w_in.shape
    o_gate = n_all - 2 * d
    n_main = n_all - n_heads
    rb = 128
    kern = functools.partial(_win_prep_kernel, o_dt=o_dt, n_heads=n_heads, o_gate=o_gate)
    return pl.pallas_call(
        kern,
        grid=(depth, k // rb),
        in_specs=[pl.BlockSpec((None, rb, n_all), lambda l, r: (l, r, 0))],
        out_specs=[
            pl.BlockSpec((None, rb, n_main), lambda l, r: (l, r, 0)),
            pl.BlockSpec((None, rb, LANE), lambda l, r: (l, r, 0)),
        ],
        out_shape=[
            jax.ShapeDtypeStruct((depth, k, n_main), BF16),
            jax.ShapeDtypeStruct((depth, k, LANE), BF16),
        ],
        compiler_params=pltpu.CompilerParams(
            dimension_semantics=("arbitrary", "arbitrary"), vmem_limit_bytes=VMEM_LIMIT),
        name="win_prep",
    )(w_in)


def kernel(x, c, ada_w, ada_b, mix_pre_g, mix_post_g, w_in, ssd_conv_w, ssd_conv_b, ssd_dt_bias, ssd_a_log,
           ssd_d, ssd_norm_g, w_ssd_out, sc_conv_w, w_sc_out, w_o, ffn_pre_g, ffn_post_g, w_up, ffn_conv_w,
           ffn_conv_b, w_down):
    bsz, seq, d = x.shape
    depth = ada_w.shape[0]
    d_inner = ssd_norm_g.shape[1]
    n_heads = ssd_dt_bias.shape[1]
    d_conv = ssd_conv_w.shape[2]
    d_ff = w_down.shape[1]
    assert d_inner == n_heads * HEAD_DIM and d_conv == d_inner + 2 * N_GROUPS * D_STATE
    assert n_heads <= LANE and seq % BLOCK == 0

    mod = _ada_call(c, ada_w, ada_b).reshape(depth, bsz, 6, d)

    tok = _token_of_row(jnp.arange(BLOCK))
    tril = (tok[:, None] >= tok[None, :]).astype(BF16)
    tril2 = jnp.concatenate([tril, tril], axis=1)
    head_of_lane = jnp.arange(d_inner) // HEAD_DIM
    expand = (jnp.arange(LANE)[:, None] == head_of_lane[None, :]).astype(BF16)
    expand2 = jnp.concatenate([expand, expand], axis=0)

    tc = _pick_tile(seq, 512)
    tm = _pick_tile(seq, 512)

    w_main_all, w_dt_all = _win_prep_call(w_in, d_inner + d_conv, n_heads, d)
    up_scale = jnp.where(jnp.arange(2 * d_ff) < d_ff, 0.5, 1.0).astype(F32)
    w_up_all = (w_up * up_scale).astype(BF16)
    w_down_all = w_down.astype(BF16)
    w_ssd_out_all = w_ssd_out.astype(BF16)
    w_sc_out_all = w_sc_out.astype(BF16)
    w_o_all = w_o.astype(BF16)

    x = _to_block_order(x)
    for i in range(depth):
        pad_h = (0, LANE - n_heads)
        dt_bias_row = jnp.pad(ssd_dt_bias[i], pad_h).reshape(1, LANE)
        a_row = jnp.pad(-jnp.exp(ssd_a_log[i]), pad_h).reshape(1, LANE)
        d_exp = jnp.repeat(ssd_d[i], HEAD_DIM).reshape(1, d_inner)

        proj, dt_raw = _inproj_call(x, mod[i], mix_pre_g[i].reshape(1, d), w_main_all, w_dt_all, i, ssd_conv_w[i],
                                    ssd_conv_b[i].reshape(1, d_conv), tm, d_inner)
        x = _ssdmix_call(proj, dt_raw, dt_bias_row, a_row, d_exp, ssd_norm_g[i].reshape(1, d_inner), tril2, expand2,
                         x, mod[i], w_ssd_out_all, sc_conv_w[i], w_sc_out_all, w_o_all, i,
                         mix_post_g[i].reshape(1, d), tc, d_inner)
        x = _ffn_call(x, mod[i], ffn_pre_g[i].reshape(1, d), w_up_all, ffn_conv_w[i],
                      ffn_conv_b[i].reshape(1, 2 * d_ff), w_down_all, i, ffn_post_g[i].reshape(1, d),
                      tm, 256)
    return _from_block_order(x)
```

```python
import functools
import math

import jax
import jax.numpy as jnp
from jax import lax
from jax.experimental import pallas as pl
from jax.experimental.pallas import tpu as pltpu

F32 = jnp.float32
BF16 = jnp.bfloat16

EPS = 1e-6
LANE = 128
SUBLANE = 8
VMEM_LIMIT = 56 * 1024 * 1024

HEAD_DIM = 64
N_GROUPS = 4
D_STATE = 128
BLOCK = 128
SEG = BLOCK // SUBLANE
SSD_CONV_K = 4
SC_CONV_K = 3
FFN_CONV_K = 3
LOG2E = math.log2(math.e)


def _dot(a, b):
    return jnp.dot(a, b, preferred_element_type=F32)


def _sigmoid(v):
    return 1.0 / (1.0 + jnp.exp(-v))


def _silu(v):
    return v * _sigmoid(v)


def _silu_of_double(hv):
    return hv + hv * jnp.tanh(hv)


def _sigmoid_of_double(hv):
    return 0.5 + 0.5 * jnp.tanh(hv)


def _split_bf16(v):
    hi = v.astype(BF16)
    lo = (v - hi.astype(F32)).astype(BF16)
    return hi, lo


def _rms(v):
    return v * lax.rsqrt(jnp.mean(v * v, axis=-1, keepdims=True) + EPS)


def _token_of_row(r):
    return (r & (SUBLANE - 1)) * SEG + (r >> 3)


def _block_shifts(cur, prev_tail, kmax):
    n = cur.shape[0]
    tail = cur[n - kmax * SUBLANE:n]
    first_sublane = lax.broadcasted_iota(jnp.int32, (SUBLANE, cur.shape[1]), 0) == 0
    wrapped = []
    for m in range(kmax):
        rows = slice(m * SUBLANE, (m + 1) * SUBLANE)
        wrapped.append(jnp.where(first_sublane, pltpu.roll(prev_tail[rows], 1, 0), pltpu.roll(tail[rows], 1, 0)))
    out = [cur]
    for k in range(1, kmax + 1):
        out.append(jnp.concatenate(wrapped[kmax - k:] + [cur[0:n - k * SUBLANE]], axis=0))
    return out


def _causal_conv(val, halo, w_ref, c0, width, kw):
    kmax = kw - 1
    outs = []
    for b0 in range(0, val.shape[0], BLOCK):
        cur = val[b0:b0 + BLOCK]
        prev_tail = halo if b0 == 0 else val[b0 - kmax * SUBLANE:b0]
        sh = _block_shifts(cur, prev_tail, kmax)
        acc = sh[0] * w_ref[kmax:kmax + 1, c0:c0 + width]
        for k in range(1, kw):
            acc = acc + sh[k] * w_ref[kmax - k:kmax - k + 1, c0:c0 + width]
        outs.append(acc)
    return outs[0] if len(outs) == 1 else jnp.concatenate(outs, axis=0)


def _ada_kernel(c_ref, w_ref, b_ref, o_ref):
    ca = _silu(c_ref[...])
    c_hi, c_lo = _split_bf16(ca)
    w_hi, w_lo = _split_bf16(w_ref[...])
    acc = _dot(c_hi, w_hi) + _dot(c_hi, w_lo) + _dot(c_lo, w_hi)
    o_ref[...] = acc + b_ref[...]


def _ada_call(c, ada_w, ada_b):
    depth, d, n = ada_w.shape
    bsz = c.shape[0]
    tn = 1024
    return pl.pallas_call(
        _ada_kernel,
        grid=(depth, n // tn),
        in_specs=[
            pl.BlockSpec((bsz, d), lambda l, j: (0, 0)),
            pl.BlockSpec((None, d, tn), lambda l, j: (l, 0, j)),
            pl.BlockSpec((None, 1, tn), lambda l, j: (l, 0, j)),
        ],
        out_specs=pl.BlockSpec((None, bsz, tn), lambda l, j: (l, 0, j)),
        out_shape=jax.ShapeDtypeStruct((depth, bsz, n), F32),
        compiler_params=pltpu.CompilerParams(
            dimension_semantics=("arbitrary", "arbitrary"), vmem_limit_bytes=VMEM_LIMIT),
        name="ada_mod",
    )(c, ada_w, ada_b.reshape(depth, 1, n))


def _inproj_kernel(x_ref, mod_ref, g_ref, w_ref, wdt_ref, cw_ref, cb_ref, o_ref, dt_ref, halo_ref,
                   *, tm, d_inner, d_conv):
    kmax = SSD_CONV_K - 1

    @pl.when(pl.program_id(1) == 0)
    def _():
        halo_ref[...] = jnp.zeros_like(halo_ref)

    h = (_rms(x_ref[...]) * (g_ref[...] * (1.0 + mod_ref[1:2, :])) + mod_ref[0:1, :]).astype(BF16)
    dt_ref[...] = _dot(h, wdt_ref[...])

    sub = 256
    starts = list(range(0, w_ref.shape[1], sub))
    heavy = [c for c in starts if d_inner <= c < d_inner + d_conv]
    plain = [c for c in starts if c >= d_inner + d_conv]
    silu_only = [c for c in starts if c < d_inner]
    light = []
    for n, c in enumerate(plain):
        light.append(c)
        light.extend(silu_only[n * len(silu_only) // len(plain):(n + 1) * len(silu_only) // len(plain)])
    order = []
    for n, c in enumerate(heavy):
        order.append(c)
        order.extend(light[n * len(light) // len(heavy):(n + 1) * len(light) // len(heavy)])
    assert sorted(order) == starts
    for c0 in order:
        cols = slice(c0, c0 + sub)
        pre = _dot(h, w_ref[:, cols])
        if c0 < d_inner:
            o_ref[:, cols] = _silu_of_double(pre).astype(BF16)
        elif c0 < d_inner + d_conv:
            cc = slice(c0 - d_inner, c0 - d_inner + sub)
            halo = halo_ref[:, cc]
            halo_ref[:, cc] = pre[tm - kmax * SUBLANE:tm]
            half_acc = _causal_conv(pre, halo, cw_ref, cc.start, sub, SSD_CONV_K) + 0.5 * cb_ref[:, cc]
            o_ref[:, cols] = _silu_of_double(half_acc).astype(BF16)
        else:
            o_ref[:, cols] = pre.astype(BF16)


def _inproj_call(x, mod3, g, w_main, w_dt, layer, conv_w, conv_b, tm, d_inner):
    bsz, seq, d = x.shape
    n = w_main.shape[2]
    slab = lambda b, i: (layer, 0, 0)
    d_conv = conv_w.shape[1]
    kern = functools.partial(_inproj_kernel, tm=tm, d_inner=d_inner, d_conv=d_conv)
    const = lambda b, i: (0, 0)
    return pl.pallas_call(
        kern,
        grid=(bsz, seq // tm),
        in_specs=[
            pl.BlockSpec((None, tm, d), lambda b, i: (b, i, 0)),
            pl.BlockSpec((None, 6, d), lambda b, i: (b, 0, 0)),
            pl.BlockSpec((1, d), const),
            pl.BlockSpec((None, d, n), slab, pipeline_mode=pl.Buffered(1)),
            pl.BlockSpec((None, d, LANE), slab),
            pl.BlockSpec((SSD_CONV_K, d_conv), const),
            pl.BlockSpec((1, d_conv), const),
        ],
        out_specs=[
            pl.BlockSpec((None, tm, n), lambda b, i: (b, i, 0)),
            pl.BlockSpec((None, tm, LANE), lambda b, i: (b, i, 0)),
        ],
        out_shape=[
            jax.ShapeDtypeStruct((bsz, seq, n), BF16),
            jax.ShapeDtypeStruct((bsz, seq, LANE), F32),
        ],
        scratch_shapes=[
            pltpu.VMEM(((SSD_CONV_K - 1) * SUBLANE, d_conv), F32),
        ],
        compiler_params=pltpu.CompilerParams(
            dimension_semantics=("parallel", "arbitrary"), vmem_limit_bytes=VMEM_LIMIT),
        name="inproj",
    )(x, mod3, g, w_main, w_dt, conv_w, conv_b)


def _ssd_chunk(r0, zs_ref, xc_ref, bc_ref, dtr_ref, dtb_ref, a_ref, dexp_ref, ng_ref, tril_ref, exp_ref, y_ref,
               state_ref, a2t_ref, causal, low_half, d_inner, after_group):
    n_heads = d_inner // HEAD_DIM
    hpg = n_heads // N_GROUPS
    gw = hpg * HEAD_DIM
    c_off = N_GROUPS * D_STATE
    rows = pl.ds(r0, BLOCK)

    dtv = dtr_ref[rows, :] + dtb_ref[...]
    dt = jnp.maximum(dtv, 0.0) + jnp.log(1.0 + jnp.exp(-jnp.abs(dtv)))
    adt_hi, adt_lo = _split_bf16(dt * a_ref[...])
    a_cs = _dot(tril_ref[...], jnp.concatenate([adt_hi, adt_lo], axis=0))
    a_last = a_cs[BLOCK - 1:BLOCK, :]
    a2 = a_cs * LOG2E
    a2t_ref[...] = a2.T

    factors = jnp.concatenate(
        [jnp.exp(a_last - a_cs), dt, jnp.broadcast_to(jnp.exp(a_last), (2 * SUBLANE, LANE))], axis=0)
    f_hi, f_lo = _split_bf16(factors)
    fac = _dot(jnp.concatenate([f_hi, f_lo], axis=1), exp_ref[...])
    dec_exp = fac[0:BLOCK, :]
    dt_exp = fac[BLOCK:2 * BLOCK, :]
    cd_exp = fac[2 * BLOCK:2 * BLOCK + 1, :]

    for g in range(N_GROUPS):
        gsl = slice(g * gw, (g + 1) * gw)
        bg = bc_ref[rows, g * D_STATE:(g + 1) * D_STATE]
        cgb = bc_ref[rows, c_off + g * D_STATE:c_off + (g + 1) * D_STATE]
        cg = cgb.astype(F32)
        scores = lax.dot_general(cgb, bg, (((1,), (1,)), ((), ())), preferred_element_type=F32)
        scores = jnp.where(causal, scores, 0.0)
        xg = xc_ref[rows, gsl].astype(F32)
        xdt = xg * dt_exp[:, gsl]
        hg = state_ref[:, gsl]
        y_pairs = []
        for j in range(hpg // 2):
            lanes = slice(j * LANE, (j + 1) * LANE)
            rhs = jnp.concatenate([xdt[:, lanes].astype(BF16), hg[:, lanes].astype(BF16)], axis=0)
            outs = []
            for q in range(2):
                r = g * hpg + 2 * j + q
                a2_col = jnp.broadcast_to(a2[:, r:r + 1], (BLOCK, BLOCK))
                decay = jnp.exp2(jnp.minimum(a2_col - a2t_ref[r:r + 1, :], 0.0))
                m = scores * decay
                ce = cg * jnp.exp2(a2_col)
                lhs = jnp.concatenate([m.astype(BF16), ce.astype(BF16)], axis=1)
                outs.append(_dot(lhs, rhs))
            y_pair = jnp.where(low_half, outs[0], outs[1])
            gl = slice(g * gw + j * LANE, g * gw + (j + 1) * LANE)
            y_pairs.append(y_pair + dexp_ref[:, gl] * xg[:, lanes])
        xw = (xdt * dec_exp[:, gsl]).astype(BF16)
        upd = _dot(bg.astype(F32).T.astype(BF16), xw)
        state_ref[:, gsl] = cd_exp[:, gsl] * hg + upd
        yg = jnp.concatenate(y_pairs, axis=1) * zs_ref[rows, gsl].astype(F32)
        y_ref[rows, gsl] = (_rms(yg) * ng_ref[:, gsl]).astype(BF16)
        after_group(g)


def _ssdmix_kernel(zs_ref, xc_ref, bc_ref, dtr_ref, dtb_ref, a_ref, dexp_ref, ng_ref, tril_ref, exp_ref,
                   x_ref, mod_ref, scb_ref, scc_ref, sch_ref, gs_ref, gc_ref,
                   wso_ref, scw_ref, wsc_ref, wo_ref, pg_ref, o_ref,
                   state_ref, a2t_ref, ycur_ref, yprev_ref, halo_ref, *, tc, d_inner, n_s):
    g = pl.program_id(0)
    d = x_ref.shape[1]

    @pl.when(g % n_s == 0)
    def _():
        state_ref[...] = jnp.zeros_like(state_ref)

    @pl.when(jnp.maximum(g - 1, 0) % n_s == 0)
    def _():
        halo_ref[...] = jnp.zeros_like(halo_ref)

    @pl.when(g == 0)
    def _():
        ycur_ref[...] = jnp.zeros_like(ycur_ref)

    yprev_ref[...] = ycur_ref[...]

    tok_r = _token_of_row(lax.broadcasted_iota(jnp.int32, (BLOCK, BLOCK), 0))
    tok_c = _token_of_row(lax.broadcasted_iota(jnp.int32, (BLOCK, BLOCK), 1))
    causal = tok_r >= tok_c
    low_half = lax.broadcasted_iota(jnp.int32, (BLOCK, LANE), 1) < HEAD_DIM

    sub = 256
    col_tiles = [slice(c0, c0 + sub) for c0 in range(0, d, sub)]
    vals = {"y_ssd": [], "y_sc": [], "mix": []}

    def ssd_out_slice(cols):
        return lambda: vals["y_ssd"].append(_dot(yprev_ref[...], wso_ref[:, cols]))

    def short_conv():
        v = scc_ref[...].astype(F32) * sch_ref[...].astype(F32)
        halo = halo_ref[...]
        halo_ref[...] = v[tc - (SC_CONV_K - 1) * SUBLANE:tc]
        conv = _causal_conv(v, halo, scw_ref, 0, d, SC_CONV_K)
        vals["u"] = (scb_ref[...].astype(F32) * conv).astype(BF16)

    def sc_out_slice(cols):
        return lambda: vals["y_sc"].append(_dot(vals["u"], wsc_ref[:, cols]))

    def merge():
        y_ssd = jnp.concatenate(vals["y_ssd"], axis=1)
        y_sc = jnp.concatenate(vals["y_sc"], axis=1)
        vals["merged"] = (_sigmoid_of_double(gs_ref[...].astype(F32)) * y_ssd
                          + _sigmoid_of_double(gc_ref[...].astype(F32)) * y_sc).astype(BF16)

    def w_o_slice(cols):
        return lambda: vals["mix"].append(_dot(vals["merged"], wo_ref[:, cols]))

    def residual():
        mix = jnp.concatenate(vals["mix"], axis=1)
        o_ref[...] = x_ref[...] + _rms(mix) * (mod_ref[2:3, :] * pg_ref[...])

    pieces = ([ssd_out_slice(c) for c in col_tiles] + [short_conv] + [sc_out_slice(c) for c in col_tiles]
              + [merge] + [w_o_slice(c) for c in col_tiles] + [residual])
    n_chunks = tc // BLOCK
    n_slots = n_chunks * N_GROUPS
    done = [0]

    def run_pieces(slot):
        upto = len(pieces) * (slot + 1) // n_slots
        while done[0] < upto:
            pieces[done[0]]()
            done[0] += 1

    for c in range(n_chunks):
        _ssd_chunk(c * BLOCK, zs_ref, xc_ref, bc_ref, dtr_ref, dtb_ref, a_ref, dexp_ref, ng_ref, tril_ref, exp_ref,
                   ycur_ref, state_ref, a2t_ref, causal, low_half, d_inner,
                   lambda g, c=c: run_pieces(c * N_GROUPS + g))


def _ssdmix_call(proj, dt_raw, dt_bias_row, a_row, d_exp, norm_g, tril2, expand2, x, mod3,
                 w_ssd_out, sc_conv_w, w_sc_out, w_o, layer, post_g, tc, d_inner):
    bsz, seq, d = x.shape
    d_bc = 2 * N_GROUPS * D_STATE
    n_s = seq // tc
    n_tiles = bsz * n_s
    kern = functools.partial(_ssdmix_kernel, tc=tc, d_inner=d_inner, n_s=n_s)
    const = lambda g: (0, 0)
    slab = lambda g: (layer, 0, 0)
    base = (2 * d_inner + d_bc) // d

    def scan_block(col):
        def index(g):
            t = jnp.minimum(g, n_tiles - 1)
            return (t // n_s, t % n_s, col)
        return index

    def mix_block(col):
        def index(g):
            t = jnp.maximum(g - 1, 0)
            return (t // n_s, t % n_s, col)
        return index

    def mix_mod(g):
        return (jnp.maximum(g - 1, 0) // n_s, 0, 0)

    one = pl.Buffered(1)
    return pl.pallas_call(
        kern,
        grid=(n_tiles + 1,),
        in_specs=[
            pl.BlockSpec((None, tc, d_inner), scan_block(0)),
            pl.BlockSpec((None, tc, d_inner), scan_block(1)),
            pl.BlockSpec((None, tc, d_bc), scan_block(2 * d_inner // d_bc)),
            pl.BlockSpec((None, tc, LANE), scan_block(0)),
            pl.BlockSpec((1, LANE), const),
            pl.BlockSpec((1, LANE), const),
            pl.BlockSpec((1, d_inner), const),
            pl.BlockSpec((1, d_inner), const),
            pl.BlockSpec((BLOCK, 2 * BLOCK), const),
            pl.BlockSpec((2 * LANE, d_inner), const),
            pl.BlockSpec((None, tc, d), mix_block(0)),
            pl.BlockSpec((None, 6, d), mix_mod),
            pl.BlockSpec((None, tc, d), mix_block(base)),
            pl.BlockSpec((None, tc, d), mix_block(base + 1)),
            pl.BlockSpec((None, tc, d), mix_block(base + 2)),
            pl.BlockSpec((None, tc, d), mix_block(base + 3)),
            pl.BlockSpec((None, tc, d), mix_block(base + 4)),
            pl.BlockSpec((None, d_inner, d), slab, pipeline_mode=one),
            pl.BlockSpec((SC_CONV_K, d), const),
            pl.BlockSpec((None, d, d), slab, pipeline_mode=one),
            pl.BlockSpec((None, d, d), slab, pipeline_mode=one),
            pl.BlockSpec((1, d), const),
        ],
        out_specs=pl.BlockSpec((None, tc, d), mix_block(0)),
        out_shape=jax.ShapeDtypeStruct((bsz, seq, d), F32),
        scratch_shapes=[
            pltpu.VMEM((D_STATE, d_inner), F32),
            pltpu.VMEM((LANE, BLOCK), F32),
            pltpu.VMEM((tc, d_inner), BF16),
            pltpu.VMEM((tc, d_inner), BF16),
            pltpu.VMEM(((SC_CONV_K - 1) * SUBLANE, d), F32),
        ],
        compiler_params=pltpu.CompilerParams(
            dimension_semantics=("arbitrary",), vmem_limit_bytes=VMEM_LIMIT),
        name="ssdmix",
    )(proj, proj, proj, dt_raw, dt_bias_row, a_row, d_exp, norm_g, tril2, expand2,
      x, mod3, proj, proj, proj, proj, proj, w_ssd_out, sc_conv_w, w_sc_out, w_o, post_g)


def _ffn_kernel(x_ref, mod_ref, pre_ref, wup_ref, cw_ref, cb_ref, wdn_ref, pg_ref, o_ref,
                halo_ref, act_ref, *, tm, d_ff, cw):
    @pl.when(pl.program_id(1) == 0)
    def _():
        halo_ref[...] = jnp.zeros_like(halo_ref)

    x = x_ref[...]
    h = (_rms(x) * (pre_ref[...] * (1.0 + mod_ref[4:5, :])) + mod_ref[3:4, :]).astype(BF16)

    def conv_cols(c0, bias_scale):
        pre = _dot(h, wup_ref[:, c0:c0 + cw])
        halo = halo_ref[:, c0:c0 + cw]
        halo_ref[:, c0:c0 + cw] = pre[tm - (FFN_CONV_K - 1) * SUBLANE:tm]
        bias = cb_ref[:, c0:c0 + cw]
        if bias_scale != 1.0:
            bias = bias_scale * bias
        return _causal_conv(pre, halo, cw_ref, c0, cw, FFN_CONV_K) + bias

    for c0 in range(0, d_ff, cw):
        half_gate = conv_cols(c0, 0.5)
        val = conv_cols(d_ff + c0, 1.0)
        act_ref[:, c0:c0 + cw] = (_silu_of_double(half_gate) * val).astype(BF16)
    f = _dot(act_ref[...], wdn_ref[...])
    o_ref[...] = x + _rms(f) * (mod_ref[5:6, :] * pg_ref[...])


def _ffn_call(x, mod3, pre_g, w_up, conv_w, conv_b, w_down, layer, post_g, tm, cw):
    bsz, seq, d = x.shape
    d_ff = w_down.shape[1]
    slab = lambda b, s: (layer, 0, 0)
    kern = functools.partial(_ffn_kernel, tm=tm, d_ff=d_ff, cw=cw)
    const = lambda b, s: (0, 0)
    return pl.pallas_call(
        kern,
        grid=(bsz, seq // tm),
        in_specs=[
            pl.BlockSpec((None, tm, d), lambda b, s: (b, s, 0)),
            pl.BlockSpec((None, 6, d), lambda b, s: (b, 0, 0)),
            pl.BlockSpec((1, d), const),
            pl.BlockSpec((None, d, 2 * d_ff), slab, pipeline_mode=pl.Buffered(1)),
            pl.BlockSpec((FFN_CONV_K, 2 * d_ff), const),
            pl.BlockSpec((1, 2 * d_ff), const),
            pl.BlockSpec((None, d_ff, d), slab, pipeline_mode=pl.Buffered(1)),
            pl.BlockSpec((1, d), const),
        ],
        out_specs=pl.BlockSpec((None, tm, d), lambda b, s: (b, s, 0)),
        out_shape=jax.ShapeDtypeStruct((bsz, seq, d), F32),
        scratch_shapes=[
            pltpu.VMEM(((FFN_CONV_K - 1) * SUBLANE, 2 * d_ff), F32),
            pltpu.VMEM((tm, d_ff), BF16),
        ],
        compiler_params=pltpu.CompilerParams(
            dimension_semantics=("parallel", "arbitrary"), vmem_limit_bytes=VMEM_LIMIT),
        name="ffn",
    )(x, mod3, pre_g, w_up, conv_w, conv_b, w_down, post_g)


def _pick_tile(n, pref):
    t = min(n, pref)
    while n % t:
        t //= 2
    return t


def _to_block_order(x):
    b, s, d = x.shape
    return x.reshape(b, s // BLOCK, SUBLANE, SEG, d).transpose(0, 1, 3, 2, 4).reshape(b, s, d)


def _from_block_order(x):
    b, s, d = x.shape
    return x.reshape(b, s // BLOCK, SEG, SUBLANE, d).transpose(0, 1, 3, 2, 4).reshape(b, s, d)


def _win_prep_kernel(w_ref, main_ref, dt_ref, *, o_dt, n_heads, o_gate):
    o_sc = o_dt + n_heads
    n_sc = o_gate - o_sc
    n_all = w_ref.shape[1]
    main_ref[:, 0:o_dt] = (0.5 * w_ref[:, 0:o_dt]).astype(BF16)
    main_ref[:, o_dt:o_dt + n_sc] = w_ref[:, o_sc:o_gate].astype(BF16)
    main_ref[:, o_dt + n_sc:] = (0.5 * w_ref[:, o_gate:n_all]).astype(BF16)
    pad = jnp.zeros((w_ref.shape[0], LANE - n_heads), F32)
    dt_ref[...] = jnp.concatenate([w_ref[:, o_dt:o_sc], pad], axis=1).astype(BF16)


def _win_prep_call(w_in, o_dt, n_heads, d):
    depth, k, n_all = w_in.shape
    o_gate = n_all - 2 * d
    n_main = n_all - n_heads
    rb = 128
    kern = functools.partial(_win_prep_kernel, o_dt=o_dt, n_heads=n_heads, o_gate=o_gate)
    return pl.pallas_call(
        kern,
        grid=(depth, k // rb),
        in_specs=[pl.BlockSpec((None, rb, n_all), lambda l, r: (l, r, 0))],
        out_specs=[
            pl.BlockSpec((None, rb, n_main), lambda l, r: (l, r, 0)),
            pl.BlockSpec((None, rb, LANE), lambda l, r: (l, r, 0)),
        ],
        out_shape=[
            jax.ShapeDtypeStruct((depth, k, n_main), BF16),
            jax.ShapeDtypeStruct((depth, k, LANE), BF16),
        ],
        compiler_params=pltpu.CompilerParams(
            dimension_semantics=("arbitrary", "arbitrary"), vmem_limit_bytes=VMEM_LIMIT),
        name="win_prep",
    )(w_in)


def kernel(x, c, ada_w, ada_b, mix_pre_g, mix_post_g, w_in, ssd_conv_w, ssd_conv_b, ssd_dt_bias, ssd_a_log,
           ssd_d, ssd_norm_g, w_ssd_out, sc_conv_w, w_sc_out, w_o, ffn_pre_g, ffn_post_g, w_up, ffn_conv_w,
           ffn_conv_b, w_down):
    bsz, seq, d = x.shape
    depth = ada_w.shape[0]
    d_inner = ssd_norm_g.shape[1]
    n_heads = ssd_dt_bias.shape[1]
    d_conv = ssd_conv_w.shape[2]
    d_ff = w_down.shape[1]
    assert d_inner == n_heads * HEAD_DIM and d_conv == d_inner + 2 * N_GROUPS * D_STATE
    assert n_heads <= LANE and seq % BLOCK == 0

    mod = _ada_call(c, ada_w, ada_b).reshape(depth, bsz, 6, d)

    tok = _token_of_row(jnp.arange(BLOCK))
    tril = (tok[:, None] >= tok[None, :]).astype(BF16)
    tril2 = jnp.concatenate([tril, tril], axis=1)
    head_of_lane = jnp.arange(d_inner) // HEAD_DIM
    expand = (jnp.arange(LANE)[:, None] == head_of_lane[None, :]).astype(BF16)
    expand2 = jnp.concatenate([expand, expand], axis=0)

    tc = _pick_tile(seq, 512)
    tm = _pick_tile(seq, 512)

    w_main_all, w_dt_all = _win_prep_call(w_in, d_inner + d_conv, n_heads, d)
    up_scale = jnp.where(jnp.arange(2 * d_ff) < d_ff, 0.5, 1.0).astype(F32)
    w_up_all = (w_up * up_scale).astype(BF16)
    w_down_all = w_down.astype(BF16)
    w_ssd_out_all = w_ssd_out.astype(BF16)
    w_sc_out_all = w_sc_out.astype(BF16)
    w_o_all = w_o.astype(BF16)

    x = _to_block_order(x)
    for i in range(depth):
        pad_h = (0, LANE - n_heads)
        dt_bias_row = jnp.pad(ssd_dt_bias[i], pad_h).reshape(1, LANE)
        a_row = jnp.pad(-jnp.exp(ssd_a_log[i]), pad_h).reshape(1, LANE)
        d_exp = jnp.repeat(ssd_d[i], HEAD_DIM).reshape(1, d_inner)

        proj, dt_raw = _inproj_call(x, mod[i], mix_pre_g[i].reshape(1, d), w_main_all, w_dt_all, i, ssd_conv_w[i],
                                    ssd_conv_b[i].reshape(1, d_conv), tm, d_inner)
        x = _ssdmix_call(proj, dt_raw, dt_bias_row, a_row, d_exp, ssd_norm_g[i].reshape(1, d_inner), tril2, expand2,
                         x, mod[i], w_ssd_out_all, sc_conv_w[i], w_sc_out_all, w_o_all, i,
                         mix_post_g[i].reshape(1, d), tc, d_inner)
        x = _ffn_call(x, mod[i], ffn_pre_g[i].reshape(1, d), w_up_all, ffn_conv_w[i],
                      ffn_conv_b[i].reshape(1, 2 * d_ff), w_down_all, i, ffn_post_g[i].reshape(1, d),
                      tm, 256)
    return _from_block_order(x)
```

```python
import functools
import math

import jax
import jax.numpy as jnp
from jax import lax
from jax.experimental import pallas as pl
from jax.experimental.pallas import tpu as pltpu

F32 = jnp.float32
BF16 = jnp.bfloat16

EPS = 1e-6
LANE = 128
SUBLANE = 8
VMEM_LIMIT = 56 * 1024 * 1024

HEAD_DIM = 64
N_GROUPS = 4
D_STATE = 128
BLOCK = 128
SEG = BLOCK // SUBLANE
SSD_CONV_K = 4
SC_CONV_K = 3
FFN_CONV_K = 3
LOG2E = math.log2(math.e)


def _dot(a, b):
    return jnp.dot(a, b, preferred_element_type=F32)


def _sigmoid(v):
    return 1.0 / (1.0 + jnp.exp(-v))


def _silu(v):
    return v * _sigmoid(v)


def _silu_of_double(hv):
    return hv + hv * jnp.tanh(hv)


def _sigmoid_of_double(hv):
    return 0.5 + 0.5 * jnp.tanh(hv)


def _split_bf16(v):
    hi = v.astype(BF16)
    lo = (v - hi.astype(F32)).astype(BF16)
    return hi, lo


def _rms(v):
    return v * lax.rsqrt(jnp.mean(v * v, axis=-1, keepdims=True) + EPS)


def _token_of_row(r):
    return (r & (SUBLANE - 1)) * SEG + (r >> 3)


def _block_shifts(cur, prev_tail, kmax):
    n = cur.shape[0]
    tail = cur[n - kmax * SUBLANE:n]
    first_sublane = lax.broadcasted_iota(jnp.int32, (SUBLANE, cur.shape[1]), 0) == 0
    wrapped = []
    for m in range(kmax):
        rows = slice(m * SUBLANE, (m + 1) * SUBLANE)
        wrapped.append(jnp.where(first_sublane, pltpu.roll(prev_tail[rows], 1, 0), pltpu.roll(tail[rows], 1, 0)))
    out = [cur]
    for k in range(1, kmax + 1):
        out.append(jnp.concatenate(wrapped[kmax - k:] + [cur[0:n - k * SUBLANE]], axis=0))
    return out


def _causal_conv(val, halo, w_ref, c0, width, kw):
    kmax = kw - 1
    outs = []
    for b0 in range(0, val.shape[0], BLOCK):
        cur = val[b0:b0 + BLOCK]
        prev_tail = halo if b0 == 0 else val[b0 - kmax * SUBLANE:b0]
        sh = _block_shifts(cur, prev_tail, kmax)
        acc = sh[0] * w_ref[kmax:kmax + 1, c0:c0 + width]
        for k in range(1, kw):
            acc = acc + sh[k] * w_ref[kmax - k:kmax - k + 1, c0:c0 + width]
        outs.append(acc)
    return outs[0] if len(outs) == 1 else jnp.concatenate(outs, axis=0)


def _ada_kernel(c_ref, w_ref, b_ref, o_ref):
    ca = _silu(c_ref[...])
    c_hi, c_lo = _split_bf16(ca)
    w_hi, w_lo = _split_bf16(w_ref[...])
    acc = _dot(c_hi, w_hi) + _dot(c_hi, w_lo) + _dot(c_lo, w_hi)
    o_ref[...] = acc + b_ref[...]


def _ada_call(c, ada_w, ada_b):
    depth, d, n = ada_w.shape
    bsz = c.shape[0]
    tn = 1024
    return pl.pallas_call(
        _ada_kernel,
        grid=(depth, n // tn),
        in_specs=[
            pl.BlockSpec((bsz, d), lambda l, j: (0, 0)),
            pl.BlockSpec((None, d, tn), lambda l, j: (l, 0, j)),
            pl.BlockSpec((None, 1, tn), lambda l, j: (l, 0, j)),
        ],
        out_specs=pl.BlockSpec((None, bsz, tn), lambda l, j: (l, 0, j)),
        out_shape=jax.ShapeDtypeStruct((depth, bsz, n), F32),
        compiler_params=pltpu.CompilerParams(
            dimension_semantics=("arbitrary", "arbitrary"), vmem_limit_bytes=VMEM_LIMIT),
        name="ada_mod",
    )(c, ada_w, ada_b.reshape(depth, 1, n))


def _inproj_kernel(x_ref, mod_ref, g_ref, w_ref, wdt_ref, cw_ref, cb_ref, o_ref, dt_ref, halo_ref,
                   *, tm, d_inner, d_conv):
    kmax = SSD_CONV_K - 1

    @pl.when(pl.program_id(1) == 0)
    def _():
        halo_ref[...] = jnp.zeros_like(halo_ref)

    h = (_rms(x_ref[...]) * (g_ref[...] * (1.0 + mod_ref[1:2, :])) + mod_ref[0:1, :]).astype(BF16)
    dt_ref[...] = _dot(h, wdt_ref[...])

    sub = 256
    starts = list(range(0, w_ref.shape[1], sub))
    heavy = [c for c in starts if d_inner <= c < d_inner + d_conv]
    plain = [c for c in starts if c >= d_inner + d_conv]
    silu_only = [c for c in starts if c < d_inner]
    light = []
    for n, c in enumerate(plain):
        light.append(c)
        light.extend(silu_only[n * len(silu_only) // len(plain):(n + 1) * len(silu_only) // len(plain)])
    order = []
    for n, c in enumerate(heavy):
        order.append(c)
        order.extend(light[n * len(light) // len(heavy):(n + 1) * len(light) // len(heavy)])
    assert sorted(order) == starts
    for c0 in order:
        cols = slice(c0, c0 + sub)
        pre = _dot(h, w_ref[:, cols])
        if c0 < d_inner:
            o_ref[:, cols] = _silu_of_double(pre).astype(BF16)
        elif c0 < d_inner + d_conv:
            cc = slice(c0 - d_inner, c0 - d_inner + sub)
            halo = halo_ref[:, cc]
            halo_ref[:, cc] = pre[tm - kmax * SUBLANE:tm]
            half_acc = _causal_conv(pre, halo, cw_ref, cc.start, sub, SSD_CONV_K) + 0.5 * cb_ref[:, cc]
            o_ref[:, cols] = _silu_of_double(half_acc).astype(BF16)
        else:
            o_ref[:, cols] = pre.astype(BF16)


def _inproj_call(x, mod3, g, w_main, w_dt, layer, conv_w, conv_b, tm, d_inner):
    bsz, seq, d = x.shape
    n = w_main.shape[2]
    slab = lambda b, i: (layer, 0, 0)
    d_conv = conv_w.shape[1]
    kern = functools.partial(_inproj_kernel, tm=tm, d_inner=d_inner, d_conv=d_conv)
    const = lambda b, i: (0, 0)
    return pl.pallas_call(
        kern,
        grid=(bsz, seq // tm),
        in_specs=[
            pl.BlockSpec((None, tm, d), lambda b, i: (b, i, 0)),
            pl.BlockSpec((None, 6, d), lambda b, i: (b, 0, 0)),
            pl.BlockSpec((1, d), const),
            pl.BlockSpec((None, d, n), slab, pipeline_mode=pl.Buffered(1)),
            pl.BlockSpec((None, d, LANE), slab),
            pl.BlockSpec((SSD_CONV_K, d_conv), const),
            pl.BlockSpec((1, d_conv), const),
        ],
        out_specs=[
            pl.BlockSpec((None, tm, n), lambda b, i: (b, i, 0)),
            pl.BlockSpec((None, tm, LANE), lambda b, i: (b, i, 0)),
        ],
        out_shape=[
            jax.ShapeDtypeStruct((bsz, seq, n), BF16),
            jax.ShapeDtypeStruct((bsz, seq, LANE), F32),
        ],
        scratch_shapes=[
            pltpu.VMEM(((SSD_CONV_K - 1) * SUBLANE, d_conv), F32),
        ],
        compiler_params=pltpu.CompilerParams(
            dimension_semantics=("parallel", "arbitrary"), vmem_limit_bytes=VMEM_LIMIT),
        name="inproj",
    )(x, mod3, g, w_main, w_dt, conv_w, conv_b)


def _ssd_chunk(r0, zs_ref, xc_ref, bc_ref, dtr_ref, dtb_ref, a_ref, dexp_ref, ng_ref, tril_ref, exp_ref, y_ref,
               state_ref, a2t_ref, dtt_ref, causal, low_half, d_inner, after_group):
    n_heads = d_inner // HEAD_DIM
    hpg = n_heads // N_GROUPS
    gw = hpg * HEAD_DIM
    c_off = N_GROUPS * D_STATE
    rows = pl.ds(r0, BLOCK)

    dtv = dtr_ref[rows, :] + dtb_ref[...]
    dt = jnp.maximum(dtv, 0.0) + jnp.log(1.0 + jnp.exp(-jnp.abs(dtv)))
    adt_hi, adt_lo = _split_bf16(dt * a_ref[...])
    a_cs = _dot(tril_ref[...], jnp.concatenate([adt_hi, adt_lo], axis=0))
    a_last = a_cs[BLOCK - 1:BLOCK, :]
    a2 = a_cs * LOG2E
    log2_dt = jnp.log2(dt)
    a2t_ref[...] = (a2 - log2_dt).T
    dtt_ref[...] = log2_dt.T

    factors = jnp.concatenate(
        [jnp.exp(a_last - a_cs) * dt, jnp.broadcast_to(jnp.exp(a_last), (2 * SUBLANE, LANE))], axis=0)
    f_hi, f_lo = _split_bf16(factors)
    fac = _dot(jnp.concatenate([f_hi, f_lo], axis=1), exp_ref[...])
    w_exp = fac[0:BLOCK, :]
    cd_exp = fac[BLOCK:BLOCK + 1, :]

    for g in range(N_GROUPS):
        gsl = slice(g * gw, (g + 1) * gw)
        bg = bc_ref[rows, g * D_STATE:(g + 1) * D_STATE]
        cgb = bc_ref[rows, c_off + g * D_STATE:c_off + (g + 1) * D_STATE]
        cg = cgb.astype(F32)
        scores = lax.dot_general(cgb, bg, (((1,), (1,)), ((), ())), preferred_element_type=F32)
        scores = jnp.where(causal, scores, 0.0)
        xgb = xc_ref[rows, gsl]
        xg = xgb.astype(F32)
        hg = state_ref[:, gsl]
        y_pairs = []
        for j in range(hpg // 2):
            lanes = slice(j * LANE, (j + 1) * LANE)
            rhs = jnp.concatenate([xgb[:, lanes], hg[:, lanes].astype(BF16)], axis=0)
            outs = []
            for q in range(2):
                r = g * hpg + 2 * j + q
                a2_col = jnp.broadcast_to(a2[:, r:r + 1], (BLOCK, BLOCK))
                decay_dt = jnp.exp2(jnp.minimum(a2_col - a2t_ref[r:r + 1, :], dtt_ref[r:r + 1, :]))
                m = scores * decay_dt
                ce = cg * jnp.exp2(a2_col)
                lhs = jnp.concatenate([m.astype(BF16), ce.astype(BF16)], axis=1)
                outs.append(_dot(lhs, rhs))
            y_pair = jnp.where(low_half, outs[0], outs[1])
            gl = slice(g * gw + j * LANE, g * gw + (j + 1) * LANE)
            y_pairs.append(y_pair + dexp_ref[:, gl] * xg[:, lanes])
        xw = (xg * w_exp[:, gsl]).astype(BF16)
        upd = _dot(bg.astype(F32).T.astype(BF16), xw)
        state_ref[:, gsl] = cd_exp[:, gsl] * hg + upd
        yg = jnp.concatenate(y_pairs, axis=1) * zs_ref[rows, gsl].astype(F32)
        y_ref[rows, gsl] = (_rms(yg) * ng_ref[:, gsl]).astype(BF16)
        after_group(g)


def _ssdmix_kernel(zs_ref, xc_ref, bc_ref, dtr_ref, dtb_ref, a_ref, dexp_ref, ng_ref, tril_ref, exp_ref,
                   x_ref, mod_ref, scb_ref, scc_ref, sch_ref, gs_ref, gc_ref,
                   wso_ref, scw_ref, wsc_ref, wo_ref, pg_ref, o_ref,
                   state_ref, a2t_ref, dtt_ref, ycur_ref, yprev_ref, halo_ref, *, tc, d_inner, n_s):
    g = pl.program_id(0)
    d = x_ref.shape[1]

    @pl.when(g % n_s == 0)
    def _():
        state_ref[...] = jnp.zeros_like(state_ref)

    @pl.when(jnp.maximum(g - 1, 0) % n_s == 0)
    def _():
        halo_ref[...] = jnp.zeros_like(halo_ref)

    @pl.when(g == 0)
    def _():
        ycur_ref[...] = jnp.zeros_like(ycur_ref)

    yprev_ref[...] = ycur_ref[...]

    tok_r = _token_of_row(lax.broadcasted_iota(jnp.int32, (BLOCK, BLOCK), 0))
    tok_c = _token_of_row(lax.broadcasted_iota(jnp.int32, (BLOCK, BLOCK), 1))
    causal = tok_r >= tok_c
    low_half = lax.broadcasted_iota(jnp.int32, (BLOCK, LANE), 1) < HEAD_DIM

    sub = 256
    col_tiles = [slice(c0, c0 + sub) for c0 in range(0, d, sub)]
    vals = {"y_ssd": [], "y_sc": [], "mix": []}

    def ssd_out_slice(cols):
        return lambda: vals["y_ssd"].append(_dot(yprev_ref[...], wso_ref[:, cols]))

    def short_conv():
        v = scc_ref[...].astype(F32) * sch_ref[...].astype(F32)
        halo = halo_ref[...]
        halo_ref[...] = v[tc - (SC_CONV_K - 1) * SUBLANE:tc]
        conv = _causal_conv(v, halo, scw_ref, 0, d, SC_CONV_K)
        vals["u"] = (scb_ref[...].astype(F32) * conv).astype(BF16)

    def sc_out_slice(cols):
        return lambda: vals["y_sc"].append(_dot(vals["u"], wsc_ref[:, cols]))

    def merge():
        y_ssd = jnp.concatenate(vals["y_ssd"], axis=1)
        y_sc = jnp.concatenate(vals["y_sc"], axis=1)
        vals["merged"] = (_sigmoid_of_double(gs_ref[...].astype(F32)) * y_ssd
                          + _sigmoid_of_double(gc_ref[...].astype(F32)) * y_sc).astype(BF16)

    def w_o_slice(cols):
        return lambda: vals["mix"].append(_dot(vals["merged"], wo_ref[:, cols]))

    def residual():
        mix = jnp.concatenate(vals["mix"], axis=1)
        o_ref[...] = x_ref[...] + _rms(mix) * (mod_ref[2:3, :] * pg_ref[...])

    pieces = ([ssd_out_slice(c) for c in col_tiles] + [short_conv] + [sc_out_slice(c) for c in col_tiles]
              + [merge] + [w_o_slice(c) for c in col_tiles] + [residual])
    n_chunks = tc // BLOCK
    n_slots = n_chunks * N_GROUPS
    done = [0]

    def run_pieces(slot):
        upto = len(pieces) * (slot + 1) // n_slots
        while done[0] < upto:
            pieces[done[0]]()
            done[0] += 1

    for c in range(n_chunks):
        _ssd_chunk(c * BLOCK, zs_ref, xc_ref, bc_ref, dtr_ref, dtb_ref, a_ref, dexp_ref, ng_ref, tril_ref, exp_ref,
                   ycur_ref, state_ref, a2t_ref, dtt_ref, causal, low_half, d_inner,
                   lambda g, c=c: run_pieces(c * N_GROUPS + g))


def _ssdmix_call(proj, dt_raw, dt_bias_row, a_row, d_exp, norm_g, tril2, expand2, x, mod3,
                 w_ssd_out, sc_conv_w, w_sc_out, w_o, layer, post_g, tc, d_inner):
    bsz, seq, d = x.shape
    d_bc = 2 * N_GROUPS * D_STATE
    n_s = seq // tc
    n_tiles = bsz * n_s
    kern = functools.partial(_ssdmix_kernel, tc=tc, d_inner=d_inner, n_s=n_s)
    const = lambda g: (0, 0)
    slab = lambda g: (layer, 0, 0)
    base = (2 * d_inner + d_bc) // d

    def scan_block(col):
        def index(g):
            t = jnp.minimum(g, n_tiles - 1)
            return (t // n_s, t % n_s, col)
        return index

    def mix_block(col):
        def index(g):
            t = jnp.maximum(g - 1, 0)
            return (t // n_s, t % n_s, col)
        return index

    def mix_mod(g):
        return (jnp.maximum(g - 1, 0) // n_s, 0, 0)

    one = pl.Buffered(1)
    return pl.pallas_call(
        kern,
        grid=(n_tiles + 1,),
        in_specs=[
            pl.BlockSpec((None, tc, d_inner), scan_block(0)),
            pl.BlockSpec((None, tc, d_inner), scan_block(1)),
            pl.BlockSpec((None, tc, d_bc), scan_block(2 * d_inner // d_bc)),
            pl.BlockSpec((None, tc, LANE), scan_block(0)),
            pl.BlockSpec((1, LANE), const),
            pl.BlockSpec((1, LANE), const),
            pl.BlockSpec((1, d_inner), const),
            pl.BlockSpec((1, d_inner), const),
            pl.BlockSpec((BLOCK, 2 * BLOCK), const),
            pl.BlockSpec((2 * LANE, d_inner), const),
            pl.BlockSpec((None, tc, d), mix_block(0)),
            pl.BlockSpec((None, 6, d), mix_mod),
            pl.BlockSpec((None, tc, d), mix_block(base)),
            pl.BlockSpec((None, tc, d), mix_block(base + 1)),
            pl.BlockSpec((None, tc, d), mix_block(base + 2)),
            pl.BlockSpec((None, tc, d), mix_block(base + 3)),
            pl.BlockSpec((None, tc, d), mix_block(base + 4)),
            pl.BlockSpec((None, d_inner, d), slab, pipeline_mode=one),
            pl.BlockSpec((SC_CONV_K, d), const),
            pl.BlockSpec((None, d, d), slab, pipeline_mode=one),
            pl.BlockSpec((None, d, d), slab, pipeline_mode=one),
            pl.BlockSpec((1, d), const),
        ],
        out_specs=pl.BlockSpec((None, tc, d), mix_block(0)),
        out_shape=jax.ShapeDtypeStruct((bsz, seq, d), F32),
        scratch_shapes=[
            pltpu.VMEM((D_STATE, d_inner), F32),
            pltpu.VMEM((LANE, BLOCK), F32),
            pltpu.VMEM((LANE, BLOCK), F32),
            pltpu.VMEM((tc, d_inner), BF16),
            pltpu.VMEM((tc, d_inner), BF16),
            pltpu.VMEM(((SC_CONV_K - 1) * SUBLANE, d), F32),
        ],
        compiler_params=pltpu.CompilerParams(
            dimension_semantics=("arbitrary",), vmem_limit_bytes=VMEM_LIMIT),
        name="ssdmix",
    )(proj, proj, proj, dt_raw, dt_bias_row, a_row, d_exp, norm_g, tril2, expand2,
      x, mod3, proj, proj, proj, proj, proj, w_ssd_out, sc_conv_w, w_sc_out, w_o, post_g)


def _ffn_kernel(x_ref, mod_ref, pre_ref, wup_ref, cw_ref, cb_ref, wdn_ref, pg_ref, o_ref,
                halo_ref, act_ref, *, tm, d_ff, cw):
    @pl.when(pl.program_id(1) == 0)
    def _():
        halo_ref[...] = jnp.zeros_like(halo_ref)

    x = x_ref[...]
    h = (_rms(x) * (pre_ref[...] * (1.0 + mod_ref[4:5, :])) + mod_ref[3:4, :]).astype(BF16)

    def conv_cols(c0, bias_scale):
        pre = _dot(h, wup_ref[:, c0:c0 + cw])
        halo = halo_ref[:, c0:c0 + cw]
        halo_ref[:, c0:c0 + cw] = pre[tm - (FFN_CONV_K - 1) * SUBLANE:tm]
        bias = cb_ref[:, c0:c0 + cw]
        if bias_scale != 1.0:
            bias = bias_scale * bias
        return _causal_conv(pre, halo, cw_ref, c0, cw, FFN_CONV_K) + bias

    for c0 in range(0, d_ff, cw):
        half_gate = conv_cols(c0, 0.5)
        val = conv_cols(d_ff + c0, 1.0)
        act_ref[:, c0:c0 + cw] = (_silu_of_double(half_gate) * val).astype(BF16)
    f = _dot(act_ref[...], wdn_ref[...])
    o_ref[...] = x + _rms(f) * (mod_ref[5:6, :] * pg_ref[...])


def _ffn_call(x, mod3, pre_g, w_up, conv_w, conv_b, w_down, layer, post_g, tm, cw):
    bsz, seq, d = x.shape
    d_ff = w_down.shape[1]
    slab = lambda b, s: (layer, 0, 0)
    kern = functools.partial(_ffn_kernel, tm=tm, d_ff=d_ff, cw=cw)
    const = lambda b, s: (0, 0)
    return pl.pallas_call(
        kern,
        grid=(bsz, seq // tm),
        in_specs=[
            pl.BlockSpec((None, tm, d), lambda b, s: (b, s, 0)),
            pl.BlockSpec((None, 6, d), lambda b, s: (b, 0, 0)),
            pl.BlockSpec((1, d), const),
            pl.BlockSpec((None, d, 2 * d_ff), slab, pipeline_mode=pl.Buffered(1)),
            pl.BlockSpec((FFN_CONV_K, 2 * d_ff), const),
            pl.BlockSpec((1, 2 * d_ff), const),
            pl.BlockSpec((None, d_ff, d), slab, pipeline_mode=pl.Buffered(1)),
            pl.BlockSpec((1, d), const),
        ],
        out_specs=pl.BlockSpec((None, tm, d), lambda b, s: (b, s, 0)),
        out_shape=jax.ShapeDtypeStruct((bsz, seq, d), F32),
        scratch_shapes=[
            pltpu.VMEM(((FFN_CONV_K - 1) * SUBLANE, 2 * d_ff), F32),
            pltpu.VMEM((tm, d_ff), BF16),
        ],
        compiler_params=pltpu.CompilerParams(
            dimension_semantics=("parallel", "arbitrary"), vmem_limit_bytes=VMEM_LIMIT),
        name="ffn",
    )(x, mod3, pre_g, w_up, conv_w, conv_b, w_down, post_g)


def _pick_tile(n, pref):
    t = min(n, pref)
    while n % t:
        t //= 2
    return t


def _to_block_order(x):
    b, s, d = x.shape
    return x.reshape(b, s // BLOCK, SUBLANE, SEG, d).transpose(0, 1, 3, 2, 4).reshape(b, s, d)


def _from_block_order(x):
    b, s, d = x.shape
    return x.reshape(b, s // BLOCK, SEG, SUBLANE, d).transpose(0, 1, 3, 2, 4).reshape(b, s, d)


def _win_prep_kernel(w_ref, main_ref, dt_ref, *, o_dt, n_heads, o_gate):
    o_sc = o_dt + n_heads
    n_sc = o_gate - o_sc
    n_all = w_ref.shape[1]
    main_ref[:, 0:o_dt] = (0.5 * w_ref[:, 0:o_dt]).astype(BF16)
    main_ref[:, o_dt:o_dt + n_sc] = w_ref[:, o_sc:o_gate].astype(BF16)
    main_ref[:, o_dt + n_sc:] = (0.5 * w_ref[:, o_gate:n_all]).astype(BF16)
    pad = jnp.zeros((w_ref.shape[0], LANE - n_heads), F32)
    dt_ref[...] = jnp.concatenate([w_ref[:, o_dt:o_sc], pad], axis=1).astype(BF16)


def _win_prep_call(w_in, o_dt, n_heads, d):
    depth, k, n_all = w_in.shape
    o_gate = n_all - 2 * d
    n_main = n_all - n_heads
    rb = 128
    kern = functools.partial(_win_prep_kernel, o_dt=o_dt, n_heads=n_heads, o_gate=o_gate)
    return pl.pallas_call(
        kern,
        grid=(depth, k // rb),
        in_specs=[pl.BlockSpec((None, rb, n_all), lambda l, r: (l, r, 0))],
        out_specs=[
            pl.BlockSpec((None, rb, n_main), lambda l, r: (l, r, 0)),
            pl.BlockSpec((None, rb, LANE), lambda l, r: (l, r, 0)),
        ],
        out_shape=[
            jax.ShapeDtypeStruct((depth, k, n_main), BF16),
            jax.ShapeDtypeStruct((depth, k, LANE), BF16),
        ],
        compiler_params=pltpu.CompilerParams(
            dimension_semantics=("arbitrary", "arbitrary"), vmem_limit_bytes=VMEM_LIMIT),
        name="win_prep",
    )(w_in)


def kernel(x, c, ada_w, ada_b, mix_pre_g, mix_post_g, w_in, ssd_conv_w, ssd_conv_b, ssd_dt_bias, ssd_a_log,
           ssd_d, ssd_norm_g, w_ssd_out, sc_conv_w, w_sc_out, w_o, ffn_pre_g, ffn_post_g, w_up, ffn_conv_w,
           ffn_conv_b, w_down):
    bsz, seq, d = x.shape
    depth = ada_w.shape[0]
    d_inner = ssd_norm_g.shape[1]
    n_heads = ssd_dt_bias.shape[1]
    d_conv = ssd_conv_w.shape[2]
    d_ff = w_down.shape[1]
    assert d_inner == n_heads * HEAD_DIM and d_conv == d_inner + 2 * N_GROUPS * D_STATE
    assert n_heads <= LANE and seq % BLOCK == 0

    mod = _ada_call(c, ada_w, ada_b).reshape(depth, bsz, 6, d)

    tok = _token_of_row(jnp.arange(BLOCK))
    tril = (tok[:, None] >= tok[None, :]).astype(BF16)
    tril2 = jnp.concatenate([tril, tril], axis=1)
    head_of_lane = jnp.arange(d_inner) // HEAD_DIM
    expand = (jnp.arange(LANE)[:, None] == head_of_lane[None, :]).astype(BF16)
    expand2 = jnp.concatenate([expand, expand], axis=0)

    tc = _pick_tile(seq, 512)
    tm = _pick_tile(seq, 512)

    w_main_all, w_dt_all = _win_prep_call(w_in, d_inner + d_conv, n_heads, d)
    up_scale = jnp.where(jnp.arange(2 * d_ff) < d_ff, 0.5, 1.0).astype(F32)
    w_up_all = (w_up * up_scale).astype(BF16)
    w_down_all = w_down.astype(BF16)
    w_ssd_out_all = w_ssd_out.astype(BF16)
    w_sc_out_all = w_sc_out.astype(BF16)
    w_o_all = w_o.astype(BF16)

    x = _to_block_order(x)
    for i in range(depth):
        pad_h = (0, LANE - n_heads)
        dt_bias_row = jnp.pad(ssd_dt_bias[i], pad_h).reshape(1, LANE)
        a_row = jnp.pad(-jnp.exp(ssd_a_log[i]), pad_h).reshape(1, LANE)
        d_exp = jnp.repeat(ssd_d[i], HEAD_DIM).reshape(1, d_inner)

        proj, dt_raw = _inproj_call(x, mod[i], mix_pre_g[i].reshape(1, d), w_main_all, w_dt_all, i, ssd_conv_w[i],
                                    ssd_conv_b[i].reshape(1, d_conv), tm, d_inner)
        x = _ssdmix_call(proj, dt_raw, dt_bias_row, a_row, d_exp, ssd_norm_g[i].reshape(1, d_inner), tril2, expand2,
                         x, mod[i], w_ssd_out_all, sc_conv_w[i], w_sc_out_all, w_o_all, i,
                         mix_post_g[i].reshape(1, d), tc, d_inner)
        x = _ffn_call(x, mod[i], ffn_pre_g[i].reshape(1, d), w_up_all, ffn_conv_w[i],
                      ffn_conv_b[i].reshape(1, 2 * d_ff), w_down_all, i, ffn_post_g[i].reshape(1, d),
                      tm, 256)
    return _from_block_order(x)
```

```python
import functools
import math

import jax
import jax.numpy as jnp
from jax import lax
from jax.experimental import pallas as pl
from jax.experimental.pallas import tpu as pltpu

F32 = jnp.float32
BF16 = jnp.bfloat16

EPS = 1e-6
LANE = 128
SUBLANE = 8
VMEM_LIMIT = 56 * 1024 * 1024

HEAD_DIM = 64
N_GROUPS = 4
D_STATE = 128
BLOCK = 128
SEG = BLOCK // SUBLANE
SSD_CONV_K = 4
SC_CONV_K = 3
FFN_CONV_K = 3
LOG2E = math.log2(math.e)


def _dot(a, b):
    return jnp.dot(a, b, preferred_element_type=F32)


def _sigmoid(v):
    return 1.0 / (1.0 + jnp.exp(-v))


def _silu(v):
    return v * _sigmoid(v)


def _silu_of_double(hv):
    return hv + hv * jnp.tanh(hv)


def _sigmoid_of_double(hv):
    return 0.5 + 0.5 * jnp.tanh(hv)


def _split_bf16(v):
    hi = v.astype(BF16)
    lo = (v - hi.astype(F32)).astype(BF16)
    return hi, lo


def _rms(v):
    return v * lax.rsqrt(jnp.mean(v * v, axis=-1, keepdims=True) + EPS)


def _token_of_row(r):
    return (r & (SUBLANE - 1)) * SEG + (r >> 3)


def _block_shifts(cur, prev_tail, kmax):
    n = cur.shape[0]
    tail = cur[n - kmax * SUBLANE:n]
    first_sublane = lax.broadcasted_iota(jnp.int32, (SUBLANE, cur.shape[1]), 0) == 0
    wrapped = []
    for m in range(kmax):
        rows = slice(m * SUBLANE, (m + 1) * SUBLANE)
        wrapped.append(jnp.where(first_sublane, pltpu.roll(prev_tail[rows], 1, 0), pltpu.roll(tail[rows], 1, 0)))
    out = [cur]
    for k in range(1, kmax + 1):
        out.append(jnp.concatenate(wrapped[kmax - k:] + [cur[0:n - k * SUBLANE]], axis=0))
    return out


def _causal_conv(val, halo, w_ref, c0, width, kw):
    kmax = kw - 1
    outs = []
    for b0 in range(0, val.shape[0], BLOCK):
        cur = val[b0:b0 + BLOCK]
        prev_tail = halo if b0 == 0 else val[b0 - kmax * SUBLANE:b0]
        sh = _block_shifts(cur, prev_tail, kmax)
        acc = sh[0] * w_ref[kmax:kmax + 1, c0:c0 + width]
        for k in range(1, kw):
            acc = acc + sh[k] * w_ref[kmax - k:kmax - k + 1, c0:c0 + width]
        outs.append(acc)
    return outs[0] if len(outs) == 1 else jnp.concatenate(outs, axis=0)


def _ada_kernel(c_ref, w_ref, b_ref, o_ref):
    ca = _silu(c_ref[...])
    c_hi, c_lo = _split_bf16(ca)
    w_hi, w_lo = _split_bf16(w_ref[...])
    acc = _dot(c_hi, w_hi) + _dot(c_hi, w_lo) + _dot(c_lo, w_hi)
    o_ref[...] = acc + b_ref[...]


def _ada_call(c, ada_w, ada_b):
    depth, d, n = ada_w.shape
    bsz = c.shape[0]
    tn = 1024
    return pl.pallas_call(
        _ada_kernel,
        grid=(depth, n // tn),
        in_specs=[
            pl.BlockSpec((bsz, d), lambda l, j: (0, 0)),
            pl.BlockSpec((None, d, tn), lambda l, j: (l, 0, j)),
            pl.BlockSpec((None, 1, tn), lambda l, j: (l, 0, j)),
        ],
        out_specs=pl.BlockSpec((None, bsz, tn), lambda l, j: (l, 0, j)),
        out_shape=jax.ShapeDtypeStruct((depth, bsz, n), F32),
        compiler_params=pltpu.CompilerParams(
            dimension_semantics=("arbitrary", "arbitrary"), vmem_limit_bytes=VMEM_LIMIT),
        name="ada_mod",
    )(c, ada_w, ada_b.reshape(depth, 1, n))


def _inproj_kernel(x_ref, mod_ref, g_ref, w_ref, wdt_ref, cw_ref, cb_ref, scw_ref, o_ref, dt_ref,
                   halo_ref, sc_halo_ref, *, tm, d_inner, d_conv):
    d = x_ref.shape[1]
    kmax = SSD_CONV_K - 1
    o_sc = d_inner + d_conv
    o_gate = o_sc + 3 * d

    @pl.when(pl.program_id(1) == 0)
    def _():
        halo_ref[...] = jnp.zeros_like(halo_ref)
        sc_halo_ref[...] = jnp.zeros_like(sc_halo_ref)

    h = (_rms(x_ref[...]) * (g_ref[...] * (1.0 + mod_ref[1:2, :])) + mod_ref[0:1, :]).astype(BF16)
    dt_ref[...] = _dot(h, wdt_ref[...])

    sub = 256

    def silu_tile(c0):
        cols = slice(c0, c0 + sub)
        o_ref[:, cols] = _silu_of_double(_dot(h, w_ref[:, cols])).astype(BF16)

    def conv_tile(c0):
        cols = slice(c0, c0 + sub)
        cc = slice(c0 - d_inner, c0 - d_inner + sub)
        pre = _dot(h, w_ref[:, cols])
        halo = halo_ref[:, cc]
        halo_ref[:, cc] = pre[tm - kmax * SUBLANE:tm]
        half_acc = _causal_conv(pre, halo, cw_ref, cc.start, sub, SSD_CONV_K) + 0.5 * cb_ref[:, cc]
        o_ref[:, cols] = _silu_of_double(half_acc).astype(BF16)

    def short_conv_tile(k0):
        cc = slice(k0, k0 + sub)
        sc_b = _dot(h, w_ref[:, o_sc + k0:o_sc + k0 + sub])
        v = _dot(h, w_ref[:, o_sc + d + k0:o_sc + d + k0 + sub]) * _dot(
            h, w_ref[:, o_sc + 2 * d + k0:o_sc + 2 * d + k0 + sub])
        halo = sc_halo_ref[:, cc]
        sc_halo_ref[:, cc] = v[tm - (SC_CONV_K - 1) * SUBLANE:tm]
        o_ref[:, o_sc + k0:o_sc + k0 + sub] = (sc_b * _causal_conv(v, halo, scw_ref, k0, sub, SC_CONV_K)).astype(BF16)

    def gate_tile(c0):
        o_ref[:, c0 - 2 * d:c0 - 2 * d + sub] = _dot(h, w_ref[:, c0:c0 + sub]).astype(BF16)

    heavy = [functools.partial(conv_tile, c) for c in range(d_inner, o_sc, sub)]
    silu_only = [functools.partial(silu_tile, c) for c in range(0, d_inner, sub)]
    plain = ([functools.partial(gate_tile, c) for c in range(o_gate, w_ref.shape[1], sub)]
             + [functools.partial(short_conv_tile, k) for k in range(0, d, sub)])
    light = []
    for n, t in enumerate(plain):
        light.append(t)
        light.extend(silu_only[n * len(silu_only) // len(plain):(n + 1) * len(silu_only) // len(plain)])
    for n, t in enumerate(heavy):
        t()
        for lt in light[n * len(light) // len(heavy):(n + 1) * len(light) // len(heavy)]:
            lt()


def _inproj_call(x, mod3, g, w_main, w_dt, layer, conv_w, conv_b, sc_conv_w, tm, d_inner):
    bsz, seq, d = x.shape
    n = w_main.shape[2]
    n_out = n - 2 * d
    slab = lambda b, i: (layer, 0, 0)
    d_conv = conv_w.shape[1]
    kern = functools.partial(_inproj_kernel, tm=tm, d_inner=d_inner, d_conv=d_conv)
    const = lambda b, i: (0, 0)
    return pl.pallas_call(
        kern,
        grid=(bsz, seq // tm),
        in_specs=[
            pl.BlockSpec((None, tm, d), lambda b, i: (b, i, 0)),
            pl.BlockSpec((None, 6, d), lambda b, i: (b, 0, 0)),
            pl.BlockSpec((1, d), const),
            pl.BlockSpec((None, d, n), slab, pipeline_mode=pl.Buffered(1)),
            pl.BlockSpec((None, d, LANE), slab),
            pl.BlockSpec((SSD_CONV_K, d_conv), const),
            pl.BlockSpec((1, d_conv), const),
            pl.BlockSpec((SC_CONV_K, d), const),
        ],
        out_specs=[
            pl.BlockSpec((None, tm, n_out), lambda b, i: (b, i, 0)),
            pl.BlockSpec((None, tm, LANE), lambda b, i: (b, i, 0)),
        ],
        out_shape=[
            jax.ShapeDtypeStruct((bsz, seq, n_out), BF16),
            jax.ShapeDtypeStruct((bsz, seq, LANE), F32),
        ],
        scratch_shapes=[
            pltpu.VMEM(((SSD_CONV_K - 1) * SUBLANE, d_conv), F32),
            pltpu.VMEM(((SC_CONV_K - 1) * SUBLANE, d), F32),
        ],
        compiler_params=pltpu.CompilerParams(
            dimension_semantics=("parallel", "arbitrary"), vmem_limit_bytes=VMEM_LIMIT),
        name="inproj",
    )(x, mod3, g, w_main, w_dt, conv_w, conv_b, sc_conv_w)


def _ssd_chunk(r0, zs_ref, xc_ref, bc_ref, dtr_ref, dtb_ref, a_ref, dexp_ref, ng_ref, tril_ref, exp_ref, y_ref,
               state_ref, a2t_ref, dtt_ref, causal, low_half, d_inner, after_group):
    n_heads = d_inner // HEAD_DIM
    hpg = n_heads // N_GROUPS
    gw = hpg * HEAD_DIM
    c_off = N_GROUPS * D_STATE
    rows = pl.ds(r0, BLOCK)

    dtv = dtr_ref[rows, :] + dtb_ref[...]
    dt = jnp.maximum(dtv, 0.0) + jnp.log(1.0 + jnp.exp(-jnp.abs(dtv)))
    adt_hi, adt_lo = _split_bf16(dt * a_ref[...])
    a_cs = _dot(tril_ref[...], jnp.concatenate([adt_hi, adt_lo], axis=0))
    a_last = a_cs[BLOCK - 1:BLOCK, :]
    a2 = a_cs * LOG2E
    log2_dt = jnp.log2(dt)
    a2t_ref[...] = (a2 - log2_dt).T
    dtt_ref[...] = log2_dt.T

    factors = jnp.concatenate(
        [jnp.exp(a_last - a_cs) * dt, jnp.broadcast_to(jnp.exp(a_last), (2 * SUBLANE, LANE))], axis=0)
    f_hi, f_lo = _split_bf16(factors)
    fac = _dot(jnp.concatenate([f_hi, f_lo], axis=1), exp_ref[...])
    w_exp = fac[0:BLOCK, :]
    cd_exp = fac[BLOCK:BLOCK + 1, :]

    for g in range(N_GROUPS):
        gsl = slice(g * gw, (g + 1) * gw)
        bg = bc_ref[rows, g * D_STATE:(g + 1) * D_STATE]
        cgb = bc_ref[rows, c_off + g * D_STATE:c_off + (g + 1) * D_STATE]
        cg = cgb.astype(F32)
        scores = lax.dot_general(cgb, bg, (((1,), (1,)), ((), ())), preferred_element_type=F32)
        scores = jnp.where(causal, scores, 0.0)
        xgb = xc_ref[rows, gsl]
        xg = xgb.astype(F32)
        hg = state_ref[:, gsl]
        y_pairs = []
        for j in range(hpg // 2):
            lanes = slice(j * LANE, (j + 1) * LANE)
            rhs = jnp.concatenate([xgb[:, lanes], hg[:, lanes].astype(BF16)], axis=0)
            outs = []
            for q in range(2):
                r = g * hpg + 2 * j + q
                a2_col = jnp.broadcast_to(a2[:, r:r + 1], (BLOCK, BLOCK))
                decay_dt = jnp.exp2(jnp.minimum(a2_col - a2t_ref[r:r + 1, :], dtt_ref[r:r + 1, :]))
                m = scores * decay_dt
                ce = cg * jnp.exp2(a2_col)
                lhs = jnp.concatenate([m.astype(BF16), ce.astype(BF16)], axis=1)
                outs.append(_dot(lhs, rhs))
            y_pair = jnp.where(low_half, outs[0], outs[1])
            gl = slice(g * gw + j * LANE, g * gw + (j + 1) * LANE)
            y_pairs.append(y_pair + dexp_ref[:, gl] * xg[:, lanes])
        xw = (xg * w_exp[:, gsl]).astype(BF16)
        upd = _dot(bg.astype(F32).T.astype(BF16), xw)
        state_ref[:, gsl] = cd_exp[:, gsl] * hg + upd
        yg = jnp.concatenate(y_pairs, axis=1) * zs_ref[rows, gsl].astype(F32)
        y_ref[rows, gsl] = (_rms(yg) * ng_ref[:, gsl]).astype(BF16)
        after_group(g)


def _ssdmix_kernel(zs_ref, xc_ref, bc_ref, dtr_ref, dtb_ref, a_ref, dexp_ref, ng_ref, tril_ref, exp_ref,
                   x_ref, mod_ref, u_ref, gs_ref, gc_ref,
                   wso_ref, wsc_ref, wo_ref, pg_ref, o_ref,
                   state_ref, a2t_ref, dtt_ref, ycur_ref, yprev_ref, *, tc, d_inner, n_s):
    g = pl.program_id(0)
    d = x_ref.shape[1]

    @pl.when(g % n_s == 0)
    def _():
        state_ref[...] = jnp.zeros_like(state_ref)

    @pl.when(g == 0)
    def _():
        ycur_ref[...] = jnp.zeros_like(ycur_ref)

    yprev_ref[...] = ycur_ref[...]

    tok_r = _token_of_row(lax.broadcasted_iota(jnp.int32, (BLOCK, BLOCK), 0))
    tok_c = _token_of_row(lax.broadcasted_iota(jnp.int32, (BLOCK, BLOCK), 1))
    causal = tok_r >= tok_c
    low_half = lax.broadcasted_iota(jnp.int32, (BLOCK, LANE), 1) < HEAD_DIM

    sub = 256
    col_tiles = [slice(c0, c0 + sub) for c0 in range(0, d, sub)]
    vals = {"y_ssd": [], "y_sc": [], "mix": []}

    def ssd_out_slice(cols):
        return lambda: vals["y_ssd"].append(_dot(yprev_ref[...], wso_ref[:, cols]))

    def sc_out_slice(cols):
        return lambda: vals["y_sc"].append(_dot(u_ref[...], wsc_ref[:, cols]))

    def merge():
        y_ssd = jnp.concatenate(vals["y_ssd"], axis=1)
        y_sc = jnp.concatenate(vals["y_sc"], axis=1)
        vals["merged"] = (_sigmoid_of_double(gs_ref[...].astype(F32)) * y_ssd
                          + _sigmoid_of_double(gc_ref[...].astype(F32)) * y_sc).astype(BF16)

    def w_o_slice(cols):
        return lambda: vals["mix"].append(_dot(vals["merged"], wo_ref[:, cols]))

    def residual():
        mix = jnp.concatenate(vals["mix"], axis=1)
        o_ref[...] = x_ref[...] + _rms(mix) * (mod_ref[2:3, :] * pg_ref[...])

    pieces = ([ssd_out_slice(c) for c in col_tiles] + [sc_out_slice(c) for c in col_tiles]
              + [merge] + [w_o_slice(c) for c in col_tiles] + [residual])
    n_chunks = tc // BLOCK
    n_slots = n_chunks * N_GROUPS
    done = [0]

    def run_pieces(slot):
        upto = len(pieces) * (slot + 1) // n_slots
        while done[0] < upto:
            pieces[done[0]]()
            done[0] += 1

    for c in range(n_chunks):
        _ssd_chunk(c * BLOCK, zs_ref, xc_ref, bc_ref, dtr_ref, dtb_ref, a_ref, dexp_ref, ng_ref, tril_ref, exp_ref,
                   ycur_ref, state_ref, a2t_ref, dtt_ref, causal, low_half, d_inner,
                   lambda g, c=c: run_pieces(c * N_GROUPS + g))


def _ssdmix_call(proj, dt_raw, dt_bias_row, a_row, d_exp, norm_g, tril2, expand2, x, mod3,
                 w_ssd_out, w_sc_out, w_o, layer, post_g, tc, d_inner):
    bsz, seq, d = x.shape
    d_bc = 2 * N_GROUPS * D_STATE
    n_s = seq // tc
    n_tiles = bsz * n_s
    kern = functools.partial(_ssdmix_kernel, tc=tc, d_inner=d_inner, n_s=n_s)
    const = lambda g: (0, 0)
    slab = lambda g: (layer, 0, 0)
    base = (2 * d_inner + d_bc) // d

    def scan_block(col):
        def index(g):
            t = jnp.minimum(g, n_tiles - 1)
            return (t // n_s, t % n_s, col)
        return index

    def mix_block(col):
        def index(g):
            t = jnp.maximum(g - 1, 0)
            return (t // n_s, t % n_s, col)
        return index

    def mix_mod(g):
        return (jnp.maximum(g - 1, 0) // n_s, 0, 0)

    one = pl.Buffered(1)
    return pl.pallas_call(
        kern,
        grid=(n_tiles + 1,),
        in_specs=[
            pl.BlockSpec((None, tc, d_inner), scan_block(0)),
            pl.BlockSpec((None, tc, d_inner), scan_block(1)),
            pl.BlockSpec((None, tc, d_bc), scan_block(2 * d_inner // d_bc)),
            pl.BlockSpec((None, tc, LANE), scan_block(0)),
            pl.BlockSpec((1, LANE), const),
            pl.BlockSpec((1, LANE), const),
            pl.BlockSpec((1, d_inner), const),
            pl.BlockSpec((1, d_inner), const),
            pl.BlockSpec((BLOCK, 2 * BLOCK), const),
            pl.BlockSpec((2 * LANE, d_inner), const),
            pl.BlockSpec((None, tc, d), mix_block(0)),
            pl.BlockSpec((None, 6, d), mix_mod),
            pl.BlockSpec((None, tc, d), mix_block(base)),
            pl.BlockSpec((None, tc, d), mix_block(base + 1)),
            pl.BlockSpec((None, tc, d), mix_block(base + 2)),
            pl.BlockSpec((None, d_inner, d), slab, pipeline_mode=one),
            pl.BlockSpec((None, d, d), slab, pipeline_mode=one),
            pl.BlockSpec((None, d, d), slab, pipeline_mode=one),
            pl.BlockSpec((1, d), const),
        ],
        out_specs=pl.BlockSpec((None, tc, d), mix_block(0)),
        out_shape=jax.ShapeDtypeStruct((bsz, seq, d), F32),
        scratch_shapes=[
            pltpu.VMEM((D_STATE, d_inner), F32),
            pltpu.VMEM((LANE, BLOCK), F32),
            pltpu.VMEM((LANE, BLOCK), F32),
            pltpu.VMEM((tc, d_inner), BF16),
            pltpu.VMEM((tc, d_inner), BF16),
        ],
        compiler_params=pltpu.CompilerParams(
            dimension_semantics=("arbitrary",), vmem_limit_bytes=VMEM_LIMIT),
        name="ssdmix",
    )(proj, proj, proj, dt_raw, dt_bias_row, a_row, d_exp, norm_g, tril2, expand2,
      x, mod3, proj, proj, proj, w_ssd_out, w_sc_out, w_o, post_g)


def _ffn_kernel(x_ref, mod_ref, pre_ref, wup_ref, cw_ref, cb_ref, wdn_ref, pg_ref, o_ref,
                halo_ref, act_ref, *, tm, d_ff, cw):
    @pl.when(pl.program_id(1) == 0)
    def _():
        halo_ref[...] = jnp.zeros_like(halo_ref)

    x = x_ref[...]
    h = (_rms(x) * (pre_ref[...] * (1.0 + mod_ref[4:5, :])) + mod_ref[3:4, :]).astype(BF16)

    def conv_cols(c0, bias_scale):
        pre = _dot(h, wup_ref[:, c0:c0 + cw])
        halo = halo_ref[:, c0:c0 + cw]
        halo_ref[:, c0:c0 + cw] = pre[tm - (FFN_CONV_K - 1) * SUBLANE:tm]
        bias = cb_ref[:, c0:c0 + cw]
        if bias_scale != 1.0:
            bias = bias_scale * bias
        return _causal_conv(pre, halo, cw_ref, c0, cw, FFN_CONV_K) + bias

    for c0 in range(0, d_ff, cw):
        half_gate = conv_cols(c0, 0.5)
        val = conv_cols(d_ff + c0, 1.0)
        act_ref[:, c0:c0 + cw] = (_silu_of_double(half_gate) * val).astype(BF16)
    f = _dot(act_ref[...], wdn_ref[...])
    o_ref[...] = x + _rms(f) * (mod_ref[5:6, :] * pg_ref[...])


def _ffn_call(x, mod3, pre_g, w_up, conv_w, conv_b, w_down, layer, post_g, tm, cw):
    bsz, seq, d = x.shape
    d_ff = w_down.shape[1]
    slab = lambda b, s: (layer, 0, 0)
    kern = functools.partial(_ffn_kernel, tm=tm, d_ff=d_ff, cw=cw)
    const = lambda b, s: (0, 0)
    return pl.pallas_call(
        kern,
        grid=(bsz, seq // tm),
        in_specs=[
            pl.BlockSpec((None, tm, d), lambda b, s: (b, s, 0)),
            pl.BlockSpec((None, 6, d), lambda b, s: (b, 0, 0)),
            pl.BlockSpec((1, d), const),
            pl.BlockSpec((None, d, 2 * d_ff), slab, pipeline_mode=pl.Buffered(1)),
            pl.BlockSpec((FFN_CONV_K, 2 * d_ff), const),
            pl.BlockSpec((1, 2 * d_ff), const),
            pl.BlockSpec((None, d_ff, d), slab, pipeline_mode=pl.Buffered(1)),
            pl.BlockSpec((1, d), const),
        ],
        out_specs=pl.BlockSpec((None, tm, d), lambda b, s: (b, s, 0)),
        out_shape=jax.ShapeDtypeStruct((bsz, seq, d), F32),
        scratch_shapes=[
            pltpu.VMEM(((FFN_CONV_K - 1) * SUBLANE, 2 * d_ff), F32),
            pltpu.VMEM((tm, d_ff), BF16),
        ],
        compiler_params=pltpu.CompilerParams(
            dimension_semantics=("parallel", "arbitrary"), vmem_limit_bytes=VMEM_LIMIT),
        name="ffn",
    )(x, mod3, pre_g, w_up, conv_w, conv_b, w_down, post_g)


def _pick_tile(n, pref):
    t = min(n, pref)
    while n % t:
        t //= 2
    return t


def _to_block_order(x):
    b, s, d = x.shape
    return x.reshape(b, s // BLOCK, SUBLANE, SEG, d).transpose(0, 1, 3, 2, 4).reshape(b, s, d)


def _from_block_order(x):
    b, s, d = x.shape
    return x.reshape(b, s // BLOCK, SEG, SUBLANE, d).transpose(0, 1, 3, 2, 4).reshape(b, s, d)


def _win_prep_kernel(w_ref, main_ref, dt_ref, *, o_dt, n_heads, o_gate):
    o_sc = o_dt + n_heads
    n_sc = o_gate - o_sc
    n_all = w_ref.shape[1]
    main_ref[:, 0:o_dt] = (0.5 * w_ref[:, 0:o_dt]).astype(BF16)
    main_ref[:, o_dt:o_dt + n_sc] = w_ref[:, o_sc:o_gate].astype(BF16)
    main_ref[:, o_dt + n_sc:] = (0.5 * w_ref[:, o_gate:n_all]).astype(BF16)
    pad = jnp.zeros((w_ref.shape[0], LANE - n_heads), F32)
    dt_ref[...] = jnp.concatenate([w_ref[:, o_dt:o_sc], pad], axis=1).astype(BF16)


def _win_prep_call(w_in, o_dt, n_heads, d):
    depth, k, n_all = w_in.shape
    o_gate = n_all - 2 * d
    n_main = n_all - n_heads
    rb = 128
    kern = functools.partial(_win_prep_kernel, o_dt=o_dt, n_heads=n_heads, o_gate=o_gate)
    return pl.pallas_call(
        kern,
        grid=(depth, k // rb),
        in_specs=[pl.BlockSpec((None, rb, n_all), lambda l, r: (l, r, 0))],
        out_specs=[
            pl.BlockSpec((None, rb, n_main), lambda l, r: (l, r, 0)),
            pl.BlockSpec((None, rb, LANE), lambda l, r: (l, r, 0)),
        ],
        out_shape=[
            jax.ShapeDtypeStruct((depth, k, n_main), BF16),
            jax.ShapeDtypeStruct((depth, k, LANE), BF16),
        ],
        compiler_params=pltpu.CompilerParams(
            dimension_semantics=("arbitrary", "arbitrary"), vmem_limit_bytes=VMEM_LIMIT),
        name="win_prep",
    )(w_in)


def kernel(x, c, ada_w, ada_b, mix_pre_g, mix_post_g, w_in, ssd_conv_w, ssd_conv_b, ssd_dt_bias, ssd_a_log,
           ssd_d, ssd_norm_g, w_ssd_out, sc_conv_w, w_sc_out, w_o, ffn_pre_g, ffn_post_g, w_up, ffn_conv_w,
           ffn_conv_b, w_down):
    bsz, seq, d = x.shape
    depth = ada_w.shape[0]
    d_inner = ssd_norm_g.shape[1]
    n_heads = ssd_dt_bias.shape[1]
    d_conv = ssd_conv_w.shape[2]
    d_ff = w_down.shape[1]
    assert d_inner == n_heads * HEAD_DIM and d_conv == d_inner + 2 * N_GROUPS * D_STATE
    assert n_heads <= LANE and seq % BLOCK == 0

    mod = _ada_call(c, ada_w, ada_b).reshape(depth, bsz, 6, d)

    tok = _token_of_row(jnp.arange(BLOCK))
    tril = (tok[:, None] >= tok[None, :]).astype(BF16)
    tril2 = jnp.concatenate([tril, tril], axis=1)
    head_of_lane = jnp.arange(d_inner) // HEAD_DIM
    expand = (jnp.arange(LANE)[:, None] == head_of_lane[None, :]).astype(BF16)
    expand2 = jnp.concatenate([expand, expand], axis=0)

    tc = _pick_tile(seq, 512)
    tm = _pick_tile(seq, 512)

    w_main_all, w_dt_all = _win_prep_call(w_in, d_inner + d_conv, n_heads, d)
    up_scale = jnp.where(jnp.arange(2 * d_ff) < d_ff, 0.5, 1.0).astype(F32)
    w_up_all = (w_up * up_scale).astype(BF16)
    w_down_all = w_down.astype(BF16)
    w_ssd_out_all = w_ssd_out.astype(BF16)
    w_sc_out_all = w_sc_out.astype(BF16)
    w_o_all = w_o.astype(BF16)

    x = _to_block_order(x)
    for i in range(depth):
        pad_h = (0, LANE - n_heads)
        dt_bias_row = jnp.pad(ssd_dt_bias[i], pad_h).reshape(1, LANE)
        a_row = jnp.pad(-jnp.exp(ssd_a_log[i]), pad_h).reshape(1, LANE)
        d_exp = jnp.repeat(ssd_d[i], HEAD_DIM).reshape(1, d_inner)

        proj, dt_raw = _inproj_call(x, mod[i], mix_pre_g[i].reshape(1, d), w_main_all, w_dt_all, i, ssd_conv_w[i],
                                    ssd_conv_b[i].reshape(1, d_conv), sc_conv_w[i], tm, d_inner)
        x = _ssdmix_call(proj, dt_raw, dt_bias_row, a_row, d_exp, ssd_norm_g[i].reshape(1, d_inner), tril2, expand2,
                         x, mod[i], w_ssd_out_all, w_sc_out_all, w_o_all, i,
                         mix_post_g[i].reshape(1, d), tc, d_inner)
        x = _ffn_call(x, mod[i], ffn_pre_g[i].reshape(1, d), w_up_all, ffn_conv_w[i],
                      ffn_conv_b[i].reshape(1, 2 * d_ff), w_down_all, i, ffn_post_g[i].reshape(1, d),
                      tm, 256)
    return _from_block_order(x)
```

```python
import functools
import math

import jax
import jax.numpy as jnp
from jax import lax
from jax.experimental import pallas as pl
from jax.experimental.pallas import tpu as pltpu

F32 = jnp.float32
BF16 = jnp.bfloat16

EPS = 1e-6
LANE = 128
SUBLANE = 8
SUBLANE_BITS = 3
MXU_WIDTH = 256
VMEM_LIMIT = 56 * 1024 * 1024
ROW_TILE = 512
ADA_COL_TILE = 2048
PREP_ROW_TILE = 256

HEAD_DIM = 64
N_GROUPS = 4
D_STATE = 128
BLOCK = 128
SEG = BLOCK // SUBLANE
SSD_CONV_K = 4
SC_CONV_K = 3
FFN_CONV_K = 3
LOG2E = math.log2(math.e)


def _dot(a, b):
    return jnp.dot(a, b, preferred_element_type=F32)


def _sigmoid(v):
    return 1.0 / (1.0 + jnp.exp(-v))


def _silu(v):
    return v * _sigmoid(v)


def _silu_of_double(hv):
    return hv + hv * jnp.tanh(hv)


def _sigmoid_of_double(hv):
    return 0.5 + 0.5 * jnp.tanh(hv)


def _split_bf16(v):
    hi = v.astype(BF16)
    lo = (v - hi.astype(F32)).astype(BF16)
    return hi, lo


def _rms(v):
    return v * lax.rsqrt(jnp.mean(v * v, axis=-1, keepdims=True) + EPS)


def _token_of_row(r):
    return (r & (SUBLANE - 1)) * SEG + (r >> SUBLANE_BITS)


def _block_shifts(cur, prev_tail, kmax):
    n = cur.shape[0]
    tail = cur[n - kmax * SUBLANE:n]
    first_sublane = lax.broadcasted_iota(jnp.int32, (SUBLANE, cur.shape[1]), 0) == 0
    wrapped = []
    for m in range(kmax):
        rows = slice(m * SUBLANE, (m + 1) * SUBLANE)
        wrapped.append(jnp.where(first_sublane, pltpu.roll(prev_tail[rows], 1, 0), pltpu.roll(tail[rows], 1, 0)))
    out = [cur]
    for k in range(1, kmax + 1):
        out.append(jnp.concatenate(wrapped[kmax - k:] + [cur[0:n - k * SUBLANE]], axis=0))
    return out


def _causal_conv(val, halo, w_ref, c0, width, kw):
    kmax = kw - 1
    outs = []
    for b0 in range(0, val.shape[0], BLOCK):
        cur = val[b0:b0 + BLOCK]
        prev_tail = halo if b0 == 0 else val[b0 - kmax * SUBLANE:b0]
        sh = _block_shifts(cur, prev_tail, kmax)
        acc = sh[0] * w_ref[kmax:kmax + 1, c0:c0 + width]
        for k in range(1, kw):
            acc = acc + sh[k] * w_ref[kmax - k:kmax - k + 1, c0:c0 + width]
        outs.append(acc)
    return outs[0] if len(outs) == 1 else jnp.concatenate(outs, axis=0)


def _ada_kernel(c_ref, w_ref, b_ref, o_ref):
    ca = _silu(c_ref[...])
    c_hi, c_lo = _split_bf16(ca)
    w_hi, w_lo = _split_bf16(w_ref[...])
    acc = _dot(c_hi, w_hi) + _dot(c_hi, w_lo) + _dot(c_lo, w_hi)
    o_ref[...] = acc + b_ref[...]


def _ada_call(c, ada_w, ada_b):
    depth, d, n = ada_w.shape
    bsz = c.shape[0]
    tn = ADA_COL_TILE
    return pl.pallas_call(
        _ada_kernel,
        grid=(depth, n // tn),
        in_specs=[
            pl.BlockSpec((bsz, d), lambda l, j: (0, 0)),
            pl.BlockSpec((None, d, tn), lambda l, j: (l, 0, j)),
            pl.BlockSpec((None, 1, tn), lambda l, j: (l, 0, j)),
        ],
        out_specs=pl.BlockSpec((None, bsz, tn), lambda l, j: (l, 0, j)),
        out_shape=jax.ShapeDtypeStruct((depth, bsz, n), F32),
        compiler_params=pltpu.CompilerParams(
            dimension_semantics=("arbitrary", "arbitrary"), vmem_limit_bytes=VMEM_LIMIT),
        name="ada_mod",
    )(c, ada_w, ada_b.reshape(depth, 1, n))


def _inproj_kernel(x_ref, mod_ref, g_ref, w_ref, wdt_ref, cw_ref, cb_ref, scw_ref, o_ref, dt_ref,
                   halo_ref, sc_halo_ref, *, tm, d_inner, d_conv):
    d = x_ref.shape[1]
    kmax = SSD_CONV_K - 1
    o_sc = d_inner + d_conv
    o_gate = o_sc + 3 * d

    @pl.when(pl.program_id(1) == 0)
    def _():
        halo_ref[...] = jnp.zeros_like(halo_ref)
        sc_halo_ref[...] = jnp.zeros_like(sc_halo_ref)

    h = (_rms(x_ref[...]) * (g_ref[...] * (1.0 + mod_ref[1:2, :])) + mod_ref[0:1, :]).astype(BF16)
    dt_ref[...] = _dot(h, wdt_ref[...])

    sub = MXU_WIDTH

    def silu_tile(c0):
        cols = slice(c0, c0 + sub)
        o_ref[:, cols] = _silu_of_double(_dot(h, w_ref[:, cols])).astype(BF16)

    def conv_tile(c0):
        cols = slice(c0, c0 + sub)
        cc = slice(c0 - d_inner, c0 - d_inner + sub)
        pre = _dot(h, w_ref[:, cols])
        halo = halo_ref[:, cc]
        halo_ref[:, cc] = pre[tm - kmax * SUBLANE:tm]
        half_acc = _causal_conv(pre, halo, cw_ref, cc.start, sub, SSD_CONV_K) + 0.5 * cb_ref[:, cc]
        o_ref[:, cols] = _silu_of_double(half_acc).astype(BF16)

    def short_conv_tile(k0):
        cc = slice(k0, k0 + sub)
        sc_b = _dot(h, w_ref[:, o_sc + k0:o_sc + k0 + sub])
        v = _dot(h, w_ref[:, o_sc + d + k0:o_sc + d + k0 + sub]) * _dot(
            h, w_ref[:, o_sc + 2 * d + k0:o_sc + 2 * d + k0 + sub])
        halo = sc_halo_ref[:, cc]
        sc_halo_ref[:, cc] = v[tm - (SC_CONV_K - 1) * SUBLANE:tm]
        o_ref[:, o_sc + k0:o_sc + k0 + sub] = (sc_b * _causal_conv(v, halo, scw_ref, k0, sub, SC_CONV_K)).astype(BF16)

    def gate_tile(c0):
        o_ref[:, c0 - 2 * d:c0 - 2 * d + sub] = _dot(h, w_ref[:, c0:c0 + sub]).astype(BF16)

    heavy = [functools.partial(conv_tile, c) for c in range(d_inner, o_sc, sub)]
    silu_only = [functools.partial(silu_tile, c) for c in range(0, d_inner, sub)]
    plain = ([functools.partial(gate_tile, c) for c in range(o_gate, w_ref.shape[1], sub)]
             + [functools.partial(short_conv_tile, k) for k in range(0, d, sub)])
    light = []
    for n, t in enumerate(plain):
        light.append(t)
        light.extend(silu_only[n * len(silu_only) // len(plain):(n + 1) * len(silu_only) // len(plain)])
    for n, t in enumerate(heavy):
        t()
        for lt in light[n * len(light) // len(heavy):(n + 1) * len(light) // len(heavy)]:
            lt()


def _inproj_call(x, mod3, g, w_main, w_dt, layer, conv_w, conv_b, sc_conv_w, tm, d_inner):
    bsz, seq, d = x.shape
    n = w_main.shape[2]
    n_out = n - 2 * d
    slab = lambda b, i: (layer, 0, 0)
    d_conv = conv_w.shape[1]
    kern = functools.partial(_inproj_kernel, tm=tm, d_inner=d_inner, d_conv=d_conv)
    const = lambda b, i: (0, 0)
    return pl.pallas_call(
        kern,
        grid=(bsz, seq // tm),
        in_specs=[
            pl.BlockSpec((None, tm, d), lambda b, i: (b, i, 0)),
            pl.BlockSpec((None, 6, d), lambda b, i: (b, 0, 0)),
            pl.BlockSpec((1, d), const),
            pl.BlockSpec((None, d, n), slab, pipeline_mode=pl.Buffered(1)),
            pl.BlockSpec((None, d, LANE), slab),
            pl.BlockSpec((SSD_CONV_K, d_conv), const),
            pl.BlockSpec((1, d_conv), const),
            pl.BlockSpec((SC_CONV_K, d), const),
        ],
        out_specs=[
            pl.BlockSpec((None, tm, n_out), lambda b, i: (b, i, 0)),
            pl.BlockSpec((None, tm, LANE), lambda b, i: (b, i, 0)),
        ],
        out_shape=[
            jax.ShapeDtypeStruct((bsz, seq, n_out), BF16),
            jax.ShapeDtypeStruct((bsz, seq, LANE), F32),
        ],
        scratch_shapes=[
            pltpu.VMEM(((SSD_CONV_K - 1) * SUBLANE, d_conv), F32),
            pltpu.VMEM(((SC_CONV_K - 1) * SUBLANE, d), F32),
        ],
        compiler_params=pltpu.CompilerParams(
            dimension_semantics=("parallel", "arbitrary"), vmem_limit_bytes=VMEM_LIMIT),
        name="inproj",
    )(x, mod3, g, w_main, w_dt, conv_w, conv_b, sc_conv_w)


def _ssd_chunk(r0, zs_ref, xc_ref, bc_ref, dtr_ref, dtb_ref, a_ref, dexp_ref, tril_ref, exp_ref, y_ref,
               state_ref, a2t_ref, dtt_ref, causal, low_half, d_inner, after_group):
    n_heads = d_inner // HEAD_DIM
    hpg = n_heads // N_GROUPS
    gw = hpg * HEAD_DIM
    c_off = N_GROUPS * D_STATE
    rows = pl.ds(r0, BLOCK)

    dtv = dtr_ref[rows, :] + dtb_ref[...]
    dt = jnp.maximum(dtv, 0.0) + jnp.log(1.0 + jnp.exp(-jnp.abs(dtv)))
    adt_hi, adt_lo = _split_bf16(dt * a_ref[...])
    a_cs = _dot(tril_ref[...], jnp.concatenate([adt_hi, adt_lo], axis=0))
    a_last = a_cs[BLOCK - 1:BLOCK, :]
    a2 = a_cs * LOG2E
    log2_dt = jnp.log2(dt)
    a2t_ref[...] = (a2 - log2_dt).T
    dtt_ref[...] = log2_dt.T

    factors = jnp.concatenate(
        [jnp.exp(a_last - a_cs) * dt, jnp.broadcast_to(jnp.exp(a_last), (2 * SUBLANE, LANE))], axis=0)
    f_hi, f_lo = _split_bf16(factors)
    fac = _dot(jnp.concatenate([f_hi, f_lo], axis=1), exp_ref[...])
    w_exp = fac[0:BLOCK, :]
    cd_exp = fac[BLOCK:BLOCK + 1, :]

    for g in range(N_GROUPS):
        gsl = slice(g * gw, (g + 1) * gw)
        bg = bc_ref[rows, g * D_STATE:(g + 1) * D_STATE]
        cgb = bc_ref[rows, c_off + g * D_STATE:c_off + (g + 1) * D_STATE]
        cg = cgb.astype(F32)
        scores = lax.dot_general(cgb, bg, (((1,), (1,)), ((), ())), preferred_element_type=F32)
        scores = jnp.where(causal, scores, 0.0)
        xgb = xc_ref[rows, gsl]
        xg = xgb.astype(F32)
        hg = state_ref[:, gsl]
        y_pairs = []
        for j in range(hpg // 2):
            lanes = slice(j * LANE, (j + 1) * LANE)
            rhs = jnp.concatenate([xgb[:, lanes], hg[:, lanes].astype(BF16)], axis=0)
            outs = []
            for q in range(2):
                r = g * hpg + 2 * j + q
                a2_col = jnp.broadcast_to(a2[:, r:r + 1], (BLOCK, BLOCK))
                decay_dt = jnp.exp2(jnp.minimum(a2_col - a2t_ref[r:r + 1, :], dtt_ref[r:r + 1, :]))
                m = scores * decay_dt
                ce = cg * jnp.exp2(a2_col)
                lhs = jnp.concatenate([m.astype(BF16), ce.astype(BF16)], axis=1)
                outs.append(_dot(lhs, rhs))
            y_pair = jnp.where(low_half, outs[0], outs[1])
            gl = slice(g * gw + j * LANE, g * gw + (j + 1) * LANE)
            y_pairs.append(y_pair + dexp_ref[:, gl] * xg[:, lanes])
        xw = (xg * w_exp[:, gsl]).astype(BF16)
        upd = _dot(bg.astype(F32).T.astype(BF16), xw)
        state_ref[:, gsl] = cd_exp[:, gsl] * hg + upd
        yg = jnp.concatenate(y_pairs, axis=1) * zs_ref[rows, gsl].astype(F32)
        y_ref[rows, gsl] = _rms(yg).astype(BF16)
        after_group(g)


def _ssdmix_kernel(zs_ref, xc_ref, bc_ref, dtr_ref, dtb_ref, a_ref, dexp_ref, tril_ref, exp_ref,
                   x_ref, mod_ref, u_ref, gs_ref, gc_ref,
                   wso_ref, wsc_ref, wo_ref, pg_ref, o_ref,
                   state_ref, a2t_ref, dtt_ref, ycur_ref, yprev_ref, *, tc, d_inner, n_s):
    g = pl.program_id(0)
    d = x_ref.shape[1]

    @pl.when(g % n_s == 0)
    def _():
        state_ref[...] = jnp.zeros_like(state_ref)

    @pl.when(g == 0)
    def _():
        ycur_ref[...] = jnp.zeros_like(ycur_ref)

    yprev_ref[...] = ycur_ref[...]

    tok_r = _token_of_row(lax.broadcasted_iota(jnp.int32, (BLOCK, BLOCK), 0))
    tok_c = _token_of_row(lax.broadcasted_iota(jnp.int32, (BLOCK, BLOCK), 1))
    causal = tok_r >= tok_c
    low_half = lax.broadcasted_iota(jnp.int32, (BLOCK, LANE), 1) < HEAD_DIM

    sub = MXU_WIDTH
    col_tiles = [slice(c0, c0 + sub) for c0 in range(0, d, sub)]
    vals = {"y_ssd": [], "y_sc": [], "mix": []}

    def ssd_out_slice(cols):
        return lambda: vals["y_ssd"].append(_dot(yprev_ref[...], wso_ref[:, cols]))

    def sc_out_slice(cols):
        return lambda: vals["y_sc"].append(_dot(u_ref[...], wsc_ref[:, cols]))

    def merge():
        y_ssd = jnp.concatenate(vals["y_ssd"], axis=1)
        y_sc = jnp.concatenate(vals["y_sc"], axis=1)
        vals["merged"] = (_sigmoid_of_double(gs_ref[...].astype(F32)) * y_ssd
                          + _sigmoid_of_double(gc_ref[...].astype(F32)) * y_sc).astype(BF16)

    def w_o_slice(cols):
        return lambda: vals["mix"].append(_dot(vals["merged"], wo_ref[:, cols]))

    def residual():
        mix = jnp.concatenate(vals["mix"], axis=1)
        o_ref[...] = x_ref[...] + _rms(mix) * (mod_ref[2:3, :] * pg_ref[...])

    pieces = ([ssd_out_slice(c) for c in col_tiles] + [sc_out_slice(c) for c in col_tiles]
              + [merge] + [w_o_slice(c) for c in col_tiles] + [residual])
    n_chunks = tc // BLOCK
    n_slots = n_chunks * N_GROUPS
    done = [0]

    def run_pieces(slot):
        upto = len(pieces) * (slot + 1) // n_slots
        while done[0] < upto:
            pieces[done[0]]()
            done[0] += 1

    for c in range(n_chunks):
        _ssd_chunk(c * BLOCK, zs_ref, xc_ref, bc_ref, dtr_ref, dtb_ref, a_ref, dexp_ref, tril_ref, exp_ref,
                   ycur_ref, state_ref, a2t_ref, dtt_ref, causal, low_half, d_inner,
                   lambda g, c=c: run_pieces(c * N_GROUPS + g))


def _ssdmix_call(proj, dt_raw, dt_bias_row, a_row, d_exp, tril2, expand2, x, mod3,
                 w_ssd_out, w_sc_out, w_o, layer, post_g, tc, d_inner):
    bsz, seq, d = x.shape
    d_bc = 2 * N_GROUPS * D_STATE
    n_s = seq // tc
    n_tiles = bsz * n_s
    kern = functools.partial(_ssdmix_kernel, tc=tc, d_inner=d_inner, n_s=n_s)
    const = lambda g: (0, 0)
    slab = lambda g: (layer, 0, 0)
    base = (2 * d_inner + d_bc) // d

    def scan_block(col):
        def index(g):
            t = jnp.minimum(g, n_tiles - 1)
            return (t // n_s, t % n_s, col)
        return index

    def mix_block(col):
        def index(g):
            t = jnp.maximum(g - 1, 0)
            return (t // n_s, t % n_s, col)
        return index

    def mix_mod(g):
        return (jnp.maximum(g - 1, 0) // n_s, 0, 0)

    one = pl.Buffered(1)
    return pl.pallas_call(
        kern,
        grid=(n_tiles + 1,),
        in_specs=[
            pl.BlockSpec((None, tc, d_inner), scan_block(0)),
            pl.BlockSpec((None, tc, d_inner), scan_block(1)),
            pl.BlockSpec((None, tc, d_bc), scan_block(2 * d_inner // d_bc)),
            pl.BlockSpec((None, tc, LANE), scan_block(0)),
            pl.BlockSpec((1, LANE), const),
            pl.BlockSpec((1, LANE), const),
            pl.BlockSpec((1, d_inner), const),
            pl.BlockSpec((BLOCK, 2 * BLOCK), const),
            pl.BlockSpec((2 * LANE, d_inner), const),
            pl.BlockSpec((None, tc, d), mix_block(0)),
            pl.BlockSpec((None, 6, d), mix_mod),
            pl.BlockSpec((None, tc, d), mix_block(base)),
            pl.BlockSpec((None, tc, d), mix_block(base + 1)),
            pl.BlockSpec((None, tc, d), mix_block(base + 2)),
            pl.BlockSpec((None, d_inner, d), slab, pipeline_mode=one),
            pl.BlockSpec((None, d, d), slab, pipeline_mode=one),
            pl.BlockSpec((None, d, d), slab, pipeline_mode=one),
            pl.BlockSpec((1, d), const),
        ],
        out_specs=pl.BlockSpec((None, tc, d), mix_block(0)),
        out_shape=jax.ShapeDtypeStruct((bsz, seq, d), F32),
        scratch_shapes=[
            pltpu.VMEM((D_STATE, d_inner), F32),
            pltpu.VMEM((LANE, BLOCK), F32),
            pltpu.VMEM((LANE, BLOCK), F32),
            pltpu.VMEM((tc, d_inner), BF16),
            pltpu.VMEM((tc, d_inner), BF16),
        ],
        compiler_params=pltpu.CompilerParams(
            dimension_semantics=("arbitrary",), vmem_limit_bytes=VMEM_LIMIT),
        name="ssdmix",
    )(proj, proj, proj, dt_raw, dt_bias_row, a_row, d_exp, tril2, expand2,
      x, mod3, proj, proj, proj, w_ssd_out, w_sc_out, w_o, post_g)


def _ffn_kernel(x_ref, mod_ref, pre_ref, wup_ref, cw_ref, cb_ref, wdn_ref, pg_ref, o_ref,
                halo_ref, act_ref, *, tm, d_ff, cw):
    @pl.when(pl.program_id(1) == 0)
    def _():
        halo_ref[...] = jnp.zeros_like(halo_ref)

    x = x_ref[...]
    h = (_rms(x) * (pre_ref[...] * (1.0 + mod_ref[4:5, :])) + mod_ref[3:4, :]).astype(BF16)

    def conv_cols(c0, bias_scale):
        pre = _dot(h, wup_ref[:, c0:c0 + cw])
        halo = halo_ref[:, c0:c0 + cw]
        halo_ref[:, c0:c0 + cw] = pre[tm - (FFN_CONV_K - 1) * SUBLANE:tm]
        bias = cb_ref[:, c0:c0 + cw]
        if bias_scale != 1.0:
            bias = bias_scale * bias
        return _causal_conv(pre, halo, cw_ref, c0, cw, FFN_CONV_K) + bias

    for c0 in range(0, d_ff, cw):
        half_gate = conv_cols(c0, 0.5)
        val = conv_cols(d_ff + c0, 1.0)
        act_ref[:, c0:c0 + cw] = (_silu_of_double(half_gate) * val).astype(BF16)
    f = _dot(act_ref[...], wdn_ref[...])
    o_ref[...] = x + _rms(f) * (mod_ref[5:6, :] * pg_ref[...])


def _ffn_call(x, mod3, pre_g, w_up, conv_w, conv_b, w_down, layer, post_g, tm, cw):
    bsz, seq, d = x.shape
    d_ff = w_down.shape[1]
    slab = lambda b, s: (layer, 0, 0)
    kern = functools.partial(_ffn_kernel, tm=tm, d_ff=d_ff, cw=cw)
    const = lambda b, s: (0, 0)
    return pl.pallas_call(
        kern,
        grid=(bsz, seq // tm),
        in_specs=[
            pl.BlockSpec((None, tm, d), lambda b, s: (b, s, 0)),
            pl.BlockSpec((None, 6, d), lambda b, s: (b, 0, 0)),
            pl.BlockSpec((1, d), const),
            pl.BlockSpec((None, d, 2 * d_ff), slab, pipeline_mode=pl.Buffered(1)),
            pl.BlockSpec((FFN_CONV_K, 2 * d_ff), const),
            pl.BlockSpec((1, 2 * d_ff), const),
            pl.BlockSpec((None, d_ff, d), slab, pipeline_mode=pl.Buffered(1)),
            pl.BlockSpec((1, d), const),
        ],
        out_specs=pl.BlockSpec((None, tm, d), lambda b, s: (b, s, 0)),
        out_shape=jax.ShapeDtypeStruct((bsz, seq, d), F32),
        scratch_shapes=[
            pltpu.VMEM(((FFN_CONV_K - 1) * SUBLANE, 2 * d_ff), F32),
            pltpu.VMEM((tm, d_ff), BF16),
        ],
        compiler_params=pltpu.CompilerParams(
            dimension_semantics=("parallel", "arbitrary"), vmem_limit_bytes=VMEM_LIMIT),
        name="ffn",
    )(x, mod3, pre_g, w_up, conv_w, conv_b, w_down, post_g)


def _pick_tile(n, pref):
    t = min(n, pref)
    while n % t:
        t //= 2
    return t


def _to_block_order(x):
    b, s, d = x.shape
    return x.reshape(b, s // BLOCK, SUBLANE, SEG, d).transpose(0, 1, 3, 2, 4).reshape(b, s, d)


def _from_block_order(x):
    b, s, d = x.shape
    return x.reshape(b, s // BLOCK, SEG, SUBLANE, d).transpose(0, 1, 3, 2, 4).reshape(b, s, d)


def _win_prep_kernel(w_ref, main_ref, dt_ref, *, o_dt, n_heads, o_gate):
    o_sc = o_dt + n_heads
    n_sc = o_gate - o_sc
    n_all = w_ref.shape[1]
    main_ref[:, 0:o_dt] = (0.5 * w_ref[:, 0:o_dt]).astype(BF16)
    main_ref[:, o_dt:o_dt + n_sc] = w_ref[:, o_sc:o_gate].astype(BF16)
    main_ref[:, o_dt + n_sc:] = (0.5 * w_ref[:, o_gate:n_all]).astype(BF16)
    pad = jnp.zeros((w_ref.shape[0], LANE - n_heads), F32)
    dt_ref[...] = jnp.concatenate([w_ref[:, o_dt:o_sc], pad], axis=1).astype(BF16)


def _win_prep_call(w_in, o_dt, n_heads, d):
    depth, k, n_all = w_in.shape
    o_gate = n_all - 2 * d
    n_main = n_all - n_heads
    rb = PREP_ROW_TILE
    kern = functools.partial(_win_prep_kernel, o_dt=o_dt, n_heads=n_heads, o_gate=o_gate)
    return pl.pallas_call(
        kern,
        grid=(depth, k // rb),
        in_specs=[pl.BlockSpec((None, rb, n_all), lambda l, r: (l, r, 0))],
        out_specs=[
            pl.BlockSpec((None, rb, n_main), lambda l, r: (l, r, 0)),
            pl.BlockSpec((None, rb, LANE), lambda l, r: (l, r, 0)),
        ],
        out_shape=[
            jax.ShapeDtypeStruct((depth, k, n_main), BF16),
            jax.ShapeDtypeStruct((depth, k, LANE), BF16),
        ],
        compiler_params=pltpu.CompilerParams(
            dimension_semantics=("arbitrary", "arbitrary"), vmem_limit_bytes=VMEM_LIMIT),
        name="win_prep",
    )(w_in)


def kernel(x, c, ada_w, ada_b, mix_pre_g, mix_post_g, w_in, ssd_conv_w, ssd_conv_b, ssd_dt_bias, ssd_a_log,
           ssd_d, ssd_norm_g, w_ssd_out, sc_conv_w, w_sc_out, w_o, ffn_pre_g, ffn_post_g, w_up, ffn_conv_w,
           ffn_conv_b, w_down):
    bsz, seq, d = x.shape
    depth = ada_w.shape[0]
    d_inner = ssd_norm_g.shape[1]
    n_heads = ssd_dt_bias.shape[1]
    d_conv = ssd_conv_w.shape[2]
    d_ff = w_down.shape[1]
    assert d_inner == n_heads * HEAD_DIM and d_conv == d_inner + 2 * N_GROUPS * D_STATE
    assert n_heads <= LANE and seq % BLOCK == 0

    mod = _ada_call(c, ada_w, ada_b).reshape(depth, bsz, 6, d)

    tok = _token_of_row(jnp.arange(BLOCK))
    tril = (tok[:, None] >= tok[None, :]).astype(BF16)
    tril2 = jnp.concatenate([tril, tril], axis=1)
    head_of_lane = jnp.arange(d_inner) // HEAD_DIM
    expand = (jnp.arange(LANE)[:, None] == head_of_lane[None, :]).astype(BF16)
    expand2 = jnp.concatenate([expand, expand], axis=0)

    tc = _pick_tile(seq, ROW_TILE)
    tm = _pick_tile(seq, ROW_TILE)

    w_main_all, w_dt_all = _win_prep_call(w_in, d_inner + d_conv, n_heads, d)
    up_scale = jnp.where(jnp.arange(2 * d_ff) < d_ff, 0.5, 1.0).astype(F32)
    w_up_all = (w_up * up_scale).astype(BF16)
    w_down_all = w_down.astype(BF16)
    w_ssd_out_all = (ssd_norm_g[:, :, None] * w_ssd_out).astype(BF16)
    w_sc_out_all = w_sc_out.astype(BF16)
    w_o_all = w_o.astype(BF16)

    x = _to_block_order(x)
    for i in range(depth):
        pad_h = (0, LANE - n_heads)
        dt_bias_row = jnp.pad(ssd_dt_bias[i], pad_h).reshape(1, LANE)
        a_row = jnp.pad(-jnp.exp(ssd_a_log[i]), pad_h).reshape(1, LANE)
        d_exp = jnp.repeat(ssd_d[i], HEAD_DIM).reshape(1, d_inner)

        proj, dt_raw = _inproj_call(x, mod[i], mix_pre_g[i].reshape(1, d), w_main_all, w_dt_all, i, ssd_conv_w[i],
                                    ssd_conv_b[i].reshape(1, d_conv), sc_conv_w[i], tm, d_inner)
        x = _ssdmix_call(proj, dt_raw, dt_bias_row, a_row, d_exp, tril2, expand2,
                         x, mod[i], w_ssd_out_all, w_sc_out_all, w_o_all, i,
                         mix_post_g[i].reshape(1, d), tc, d_inner)
        x = _ffn_call(x, mod[i], ffn_pre_g[i].reshape(1, d), w_up_all, ffn_conv_w[i],
                      ffn_conv_b[i].reshape(1, 2 * d_ff), w_down_all, i, ffn_post_g[i].reshape(1, d),
                      tm, MXU_WIDTH)
    return _from_block_order(x)
```
